```python
import jax, jax.numpy as jnp
from jax import lax
import numpy as np

D_MODEL = 1024
BATCH = 2
SEQ = 16384
DEPTH = 2

GRID_W = 64
ROPE_THETA = 10000.0

ATTN_HEADS = 8
ATTN_KV_HEADS = 2
ATTN_HEAD_DIM = 64
ATTN_WIDTH = ATTN_HEADS * ATTN_HEAD_DIM
KV_WIDTH = ATTN_KV_HEADS * ATTN_HEAD_DIM
Q_BLOCK = 128

CONV_CH = D_MODEL - ATTN_WIDTH
CONV_WIDTH = 31
CONV_PAD = CONV_WIDTH // 2
L0_IN = ATTN_WIDTH + 2 * KV_WIDTH + 2 * CONV_CH

RET_HEADS = 4
RET_KEY_DIM = D_MODEL // RET_HEADS
RET_VAL_DIM = 2 * RET_KEY_DIM
RET_QK_WIDTH = RET_HEADS * RET_KEY_DIM
RET_V_WIDTH = RET_HEADS * RET_VAL_DIM
L1_IN = 2 * RET_QK_WIDTH + 2 * RET_V_WIDTH
RET_CHUNK = 128

FFN_HIDDEN = -(-8 * D_MODEL // (3 * 256)) * 256

DEEPNORM_ALPHA = (2 * DEPTH) ** 0.25
DEEPNORM_BETA = (8 * DEPTH) ** -0.25
LN_EPS = 1e-5
RMS_EPS = 1e-6

kernel_name = 'hybrid_attn_conv_retention_encoder'


def layer_norm(x, g, b, eps=LN_EPS):
    xf = x.astype(jnp.float32)
    mu = jnp.mean(xf, axis=-1, keepdims=True)
    var = jnp.mean(jnp.square(xf - mu), axis=-1, keepdims=True)
    y = (xf - mu) * lax.rsqrt(var + eps)
    return (y * g.astype(jnp.float32) + b.astype(jnp.float32)).astype(x.dtype)


def rms_norm(x, g, eps=RMS_EPS):
    xf = x.astype(jnp.float32)
    y = xf * lax.rsqrt(jnp.mean(jnp.square(xf), axis=-1, keepdims=True) + eps)
    return (y * g.astype(jnp.float32)).astype(x.dtype)


def axial_rope(seq_len, head_dim):
    rows = seq_len // GRID_W
    row = jnp.broadcast_to(jnp.arange(rows, dtype=jnp.float32)[:, None], (rows, GRID_W)).reshape(seq_len)
    col = jnp.broadcast_to(jnp.arange(GRID_W, dtype=jnp.float32)[None, :], (rows, GRID_W)).reshape(seq_len)
    axis_dim = head_dim // 2
    inv_freq = ROPE_THETA ** (-jnp.arange(0, axis_dim, 2, dtype=jnp.float32) / axis_dim)
    ang = jnp.concatenate([row[:, None] * inv_freq, col[:, None] * inv_freq], axis=-1)
    return jnp.cos(ang), jnp.sin(ang)


def apply_rope(x, cos, sin):
    half = x.shape[-1] // 2
    xf = x.astype(jnp.float32)
    x1, x2 = xf[..., :half], xf[..., half:]
    c = cos[None, :, None, :]
    s = sin[None, :, None, :]
    return jnp.concatenate([x1 * c - x2 * s, x1 * s + x2 * c], axis=-1).astype(x.dtype)


def block_attention(q, k, v):
    B, S = q.shape[0], q.shape[1]
    G = ATTN_HEADS // ATTN_KV_HEADS
    nblk = S // Q_BLOCK
    qb = q.reshape(B, nblk, Q_BLOCK, ATTN_KV_HEADS, G, ATTN_HEAD_DIM).transpose(1, 0, 2, 3, 4, 5)
    scale = ATTN_HEAD_DIM ** -0.5

    def one_block(q_blk):
        s = jnp.einsum('bqkgd,bskd->bkgqs', q_blk, k).astype(jnp.float32) * scale
        p = jax.nn.softmax(s, axis=-1).astype(v.dtype)
        return jnp.einsum('bkgqs,bskd->bqkgd', p, v)

    o = lax.map(one_block, qb)
    return o.transpose(1, 0, 2, 3, 4, 5).reshape(B, S, ATTN_WIDTH)


def conformer_conv(u, dw_w, dw_b, norm_g, norm_b):
    a, gate = jnp.split(u, 2, axis=-1)
    z = a * jax.nn.sigmoid(gate)
    z = lax.conv_general_dilated(
        z, dw_w[:, None, :].astype(z.dtype), window_strides=(1,),
        padding=[(CONV_PAD, CONV_PAD)], dimension_numbers=('NWC', 'WIO', 'NWC'),
        feature_group_count=CONV_CH) + dw_b
    z = layer_norm(z, norm_g, norm_b)
    return jax.nn.silu(z)


def attn_conv_mixer(x, w_in, q_norm_g, k_norm_g, dw_w, dw_b, conv_norm_g, conv_norm_b, w_out, cos, sin):
    B, S, _ = x.shape
    proj = x @ w_in
    q, k, v, u = jnp.split(proj, [ATTN_WIDTH, ATTN_WIDTH + KV_WIDTH, ATTN_WIDTH + 2 * KV_WIDTH], axis=-1)
    q = q.reshape(B, S, ATTN_HEADS, ATTN_HEAD_DIM)
    k = k.reshape(B, S, ATTN_KV_HEADS, ATTN_HEAD_DIM)
    v = v.reshape(B, S, ATTN_KV_HEADS, ATTN_HEAD_DIM)
    q = apply_rope(rms_norm(q, q_norm_g), cos, sin)
    k = apply_rope(rms_norm(k, k_norm_g), cos, sin)
    attn = block_attention(q, k, v)
    conv = conformer_conv(u, dw_w, dw_b, conv_norm_g, conv_norm_b)
    return jnp.concatenate([attn, conv], axis=-1) @ w_out


def retention_scan(q, k, v, log_gamma, strict):
    B, H, S, dk = q.shape
    dv = v.shape[-1]
    C = RET_CHUNK
    nC = S // C
    lg = log_gamma.astype(jnp.float32)
    pos = jnp.arange(C, dtype=jnp.float32)
    diff = pos[:, None] - pos[None, :]
    mask = (diff > 0) if strict else (diff >= 0)
    intra = jnp.where(mask[None], jnp.exp(lg[:, None, None] * jnp.where(mask, diff, 0.0)[None]), 0.0)
    q_decay = jnp.exp(lg[:, None] * (pos + 1.0)[None])
    k_decay = jnp.exp(lg[:, None] * (C - 1.0 - pos)[None])
    chunk_decay = jnp.exp(lg * C)

    def to_chunks(t):
        return t.reshape(B, H, nC, C, t.shape[-1]).transpose(2, 0, 1, 3, 4)

    def step(state, inp):
        qi, ki, vi = inp
        scores = jnp.einsum('bhqd,bhkd->bhqk', qi, ki) * intra[None]
        o = jnp.einsum('bhqk,bhkv->bhqv', scores, vi) + jnp.einsum(
            'bhqd,bhdv->bhqv', qi * q_decay[None, :, :, None], state)
        state = state * chunk_decay[None, :, None, None] + jnp.einsum(
            'bhkd,bhkv->bhdv', ki * k_decay[None, :, :, None], vi)
        return state, o

    state0 = jnp.zeros((B, H, dk, dv), jnp.float32)
    _, o = lax.scan(step, state0, (to_chunks(q), to_chunks(k), to_chunks(v)))
    return o.transpose(1, 2, 0, 3, 4).reshape(B, H, S, dv)


def retention_mixer(x, w_in, log_decay_fwd, log_decay_bwd, ret_norm_g, w_out, cos, sin):
    B, S, _ = x.shape
    proj = x @ w_in
    q, k, v, g = jnp.split(proj, [RET_QK_WIDTH, 2 * RET_QK_WIDTH, 2 * RET_QK_WIDTH + RET_V_WIDTH], axis=-1)
    q = apply_rope(q.reshape(B, S, RET_HEADS, RET_KEY_DIM), cos, sin)
    k = apply_rope(k.reshape(B, S, RET_HEADS, RET_KEY_DIM), cos, sin)
    v = v.reshape(B, S, RET_HEADS, RET_VAL_DIM)
    qh = q.transpose(0, 2, 1, 3).astype(jnp.float32)
    kh = k.transpose(0, 2, 1, 3).astype(jnp.float32) * (RET_KEY_DIM ** -0.5)
    vh = v.transpose(0, 2, 1, 3).astype(jnp.float32)
    o_fwd = retention_scan(qh, kh, vh, log_decay_fwd, strict=False)
    o_bwd = jnp.flip(retention_scan(jnp.flip(qh, 2), jnp.flip(kh, 2), jnp.flip(vh, 2),
                                    log_decay_bwd, strict=True), 2)
    y = (o_fwd + o_bwd).transpose(0, 2, 1, 3)
    mu = jnp.mean(y, axis=-1, keepdims=True)
    var = jnp.mean(jnp.square(y - mu), axis=-1, keepdims=True)
    y = ((y - mu) * lax.rsqrt(var + LN_EPS)).reshape(B, S, RET_V_WIDTH)
    y = (y * ret_norm_g.astype(jnp.float32)).astype(x.dtype)
    return (jax.nn.silu(g) * y) @ w_out


def swiglu(x, w_gate, w_up, w_down):
    return (jax.nn.silu(x @ w_gate) * (x @ w_up)) @ w_down


def setup_inputs(seed: int = 0) -> dict:
    key = jax.random.key(seed)
    ks = iter(jax.random.split(key, 40))

    def normal(shape, scale):
        return jax.random.normal(next(ks), shape, jnp.float32) * scale

    def gain(n):
        return 1.0 + normal((n,), 0.02)

    base_decay = jnp.asarray(np.log(1.0 - 2.0 ** (-5.0 - np.arange(RET_HEADS))), jnp.float32)
    d = D_MODEL
    inp = {}
    inp['x'] = normal((BATCH, SEQ, d), 1.0)
    inp['l0_w_in'] = normal((d, L0_IN), d ** -0.5)
    inp['l0_q_norm_g'] = gain(ATTN_HEAD_DIM)
    inp['l0_k_norm_g'] = gain(ATTN_HEAD_DIM)
    inp['l0_dw_w'] = normal((CONV_WIDTH, CONV_CH), CONV_WIDTH ** -0.5)
    inp['l0_dw_b'] = normal((CONV_CH,), 0.02)
    inp['l0_conv_norm_g'] = gain(CONV_CH)
    inp['l0_conv_norm_b'] = normal((CONV_CH,), 0.02)
    inp['l0_w_out'] = normal((ATTN_WIDTH + CONV_CH, d), (ATTN_WIDTH + CONV_CH) ** -0.5 * DEEPNORM_BETA)
    inp['l0_ln_mix_g'] = gain(d)
    inp['l0_ln_mix_b'] = normal((d,), 0.02)
    inp['l0_ffn_w_gate'] = normal((d, FFN_HIDDEN), d ** -0.5)
    inp['l0_ffn_w_up'] = normal((d, FFN_HIDDEN), d ** -0.5)
    inp['l0_ffn_w_down'] = normal((FFN_HIDDEN, d), FFN_HIDDEN ** -0.5 * DEEPNORM_BETA)
    inp['l0_ln_ffn_g'] = gain(d)
    inp['l0_ln_ffn_b'] = normal((d,), 0.02)
    inp['l1_w_in'] = normal((d, L1_IN), d ** -0.5)
    inp['l1_log_decay_fwd'] = base_decay * (1.0 + normal((RET_HEADS,), 0.05))
    inp['l1_log_decay_bwd'] = base_decay * (1.0 + normal((RET_HEADS,), 0.05))
    inp['l1_ret_norm_g'] = gain(RET_V_WIDTH)
    inp['l1_w_out'] = normal((RET_V_WIDTH, d), RET_V_WIDTH ** -0.5 * DEEPNORM_BETA)
    inp['l1_ln_mix_g'] = gain(d)
    inp['l1_ln_mix_b'] = normal((d,), 0.02)
    inp['l1_ffn_w_gate'] = normal((d, FFN_HIDDEN), d ** -0.5)
    inp['l1_ffn_w_up'] = normal((d, FFN_HIDDEN), d ** -0.5)
    inp['l1_ffn_w_down'] = normal((FFN_HIDDEN, d), FFN_HIDDEN ** -0.5 * DEEPNORM_BETA)
    inp['l1_ln_ffn_g'] = gain(d)
    inp['l1_ln_ffn_b'] = normal((d,), 0.02)
    return inp


def reference(x, l0_w_in, l0_q_norm_g, l0_k_norm_g, l0_dw_w, l0_dw_b, l0_conv_norm_g, l0_conv_norm_b,
              l0_w_out, l0_ln_mix_g, l0_ln_mix_b, l0_ffn_w_gate, l0_ffn_w_up, l0_ffn_w_down,
              l0_ln_ffn_g, l0_ln_ffn_b, l1_w_in, l1_log_decay_fwd, l1_log_decay_bwd, l1_ret_norm_g,
              l1_w_out, l1_ln_mix_g, l1_ln_mix_b, l1_ffn_w_gate, l1_ffn_w_up, l1_ffn_w_down,
              l1_ln_ffn_g, l1_ln_ffn_b):
    S = x.shape[1]
    cos_a, sin_a = axial_rope(S, ATTN_HEAD_DIM)
    cos_r, sin_r = axial_rope(S, RET_KEY_DIM)

    mixer_params = (
        (l0_w_in, l0_q_norm_g, l0_k_norm_g, l0_dw_w, l0_dw_b, l0_conv_norm_g, l0_conv_norm_b, l0_w_out),
        (l1_w_in, l1_log_decay_fwd, l1_log_decay_bwd, l1_ret_norm_g, l1_w_out),
    )
    mix_norms = ((l0_ln_mix_g, l0_ln_mix_b), (l1_ln_mix_g, l1_ln_mix_b))
    ffn_params = ((l0_ffn_w_gate, l0_ffn_w_up, l0_ffn_w_down), (l1_ffn_w_gate, l1_ffn_w_up, l1_ffn_w_down))
    ffn_norms = ((l0_ln_ffn_g, l0_ln_ffn_b), (l1_ln_ffn_g, l1_ln_ffn_b))

    for layer in range(DEPTH):
        if layer % 2 == 0:
            m = attn_conv_mixer(x, *mixer_params[layer], cos_a, sin_a)
        else:
            m = retention_mixer(x, *mixer_params[layer], cos_r, sin_r)
        x = layer_norm(DEEPNORM_ALPHA * x + m, *mix_norms[layer])
        x = layer_norm(DEEPNORM_ALPHA * x + swiglu(x, *ffn_params[layer]), *ffn_norms[layer])
    return x
```

```python
import functools

import jax
import jax.numpy as jnp
from jax import lax
from jax.experimental import pallas as pl
from jax.experimental.pallas import tpu as pltpu

F32 = jnp.float32
BF16 = jnp.bfloat16

GRID_W = 64
ROPE_THETA = 10000.0
ATTN_HEADS = 8
ATTN_KV_HEADS = 2
ATTN_GROUP = ATTN_HEADS // ATTN_KV_HEADS
HEAD_DIM = 64
ATTN_WIDTH = ATTN_HEADS * HEAD_DIM
KV_WIDTH = ATTN_KV_HEADS * HEAD_DIM
CONV_CH = 512
CONV_TAPS = 31
CONV_PAD = CONV_TAPS // 2
RET_HEADS = 4
RET_DK = 256
RET_DV = 512
RET_QK_WIDTH = RET_HEADS * RET_DK
RET_V_WIDTH = RET_HEADS * RET_DV
DEPTH = 2
DEEPNORM_ALPHA = (2 * DEPTH) ** 0.25
LN_EPS = 1e-5
RMS_EPS = 1e-6

LANES = 128
SUBLANES = 8
BF16_ROWS = 16
VMEM_LIMIT_BYTES = 56 * 1024 * 1024

ROW_TILE = 512
ATTN_TQ = 256
ATTN_TK = 512
V_ROWS = HEAD_DIM + BF16_ROWS
CONV_HALO = 16
CONV_RB = 32
RET_CHUNK = 256
RET_STEP = 2048
FFN_HC = 256


def _dot(a, b):
    return jnp.dot(a, b, preferred_element_type=F32)


def _layer_norm(y, g, b):
    mu = jnp.mean(y, axis=-1, keepdims=True)
    d = y - mu
    var = jnp.mean(d * d, axis=-1, keepdims=True)
    return d * lax.rsqrt(var + LN_EPS) * g + b


def _swish(t):
    return t * jax.nn.sigmoid(t)


def _params(*semantics):
    return pltpu.CompilerParams(dimension_semantics=semantics, vmem_limit_bytes=VMEM_LIMIT_BYTES)


def _resident(shape):
    nd = len(shape)
    return pl.BlockSpec(shape, lambda *_: (0,) * nd, pipeline_mode=pl.Buffered(1))


def _l0_in_kernel(x_ref, w_ref, gq_ref, gk_ref, cos_ref, sin_ref, grp_ref,
                  qT_ref, k_ref, vT_ref, z_ref):
    tm = x_ref.shape[0]
    xb = x_ref[...].astype(BF16)
    cos = cos_ref[...]
    sin = sin_ref[...]
    lane = lax.broadcasted_iota(jnp.int32, (tm, LANES), 1)
    first_half = (lane % HEAD_DIM) < (HEAD_DIM // 2)

    def norm_rope(t, g, grp):
        t2 = t * t
        hi = t2.astype(BF16)
        lo = (t2 - hi.astype(F32)).astype(BF16)
        ss = _dot(hi, grp) + _dot(lo, grp)
        tn = t * lax.rsqrt(ss * (1.0 / HEAD_DIM) + RMS_EPS) * g
        outs = []
        for j in range(t.shape[1] // LANES):
            c = tn[:, j * LANES:(j + 1) * LANES]
            partner = jnp.where(first_half,
                                pltpu.roll(c, LANES - HEAD_DIM // 2, 1),
                                pltpu.roll(c, HEAD_DIM // 2, 1))
            outs.append(c * cos + partner * sin)
        return outs[0] if len(outs) == 1 else jnp.concatenate(outs, axis=1)

    c0 = 0
    q = _dot(xb, w_ref[:, c0:c0 + ATTN_WIDTH]); c0 += ATTN_WIDTH
    k = _dot(xb, w_ref[:, c0:c0 + KV_WIDTH]); c0 += KV_WIDTH
    v = _dot(xb, w_ref[:, c0:c0 + KV_WIDTH]); c0 += KV_WIDTH
    a = _dot(xb, w_ref[:, c0:c0 + CONV_CH]); c0 += CONV_CH
    gate = _dot(xb, w_ref[:, c0:c0 + CONV_CH])

    qr = norm_rope(q, gq_ref[...], grp_ref[...]) * (HEAD_DIM ** -0.5)
    qT_ref[...] = qr.T.astype(BF16)
    k_ref[...] = norm_rope(k, gk_ref[...], grp_ref[:KV_WIDTH, :KV_WIDTH]).astype(BF16)
    vT = v.T
    ones = jnp.ones((BF16_ROWS, tm), F32)
    for kk in range(ATTN_KV_HEADS):
        vT_ref[kk] = jnp.concatenate(
            [vT[kk * HEAD_DIM:(kk + 1) * HEAD_DIM], ones], axis=0).astype(BF16)
    z_ref[...] = a * jax.nn.sigmoid(gate)


def _l0_in_proj(x, w, gq, gk, cos, sin, grp):
    B, S, D = x.shape
    tm = min(ROW_TILE, S)
    nS = S // tm
    n_in = w.shape[1]
    return pl.pallas_call(
        _l0_in_kernel,
        grid=(B, nS),
        in_specs=[
            pl.BlockSpec((None, tm, D), lambda b, i: (b, i, 0)),
            _resident((D, n_in)),
            _resident((1, ATTN_WIDTH)),
            _resident((1, KV_WIDTH)),
            pl.BlockSpec((tm, LANES), lambda b, i: (i, 0)),
            pl.BlockSpec((tm, LANES), lambda b, i: (i, 0)),
            _resident((ATTN_WIDTH, ATTN_WIDTH)),
        ],
        out_specs=[
            pl.BlockSpec((None, ATTN_WIDTH, tm), lambda b, i: (b, 0, i)),
            pl.BlockSpec((None, tm, KV_WIDTH), lambda b, i: (b, i, 0)),
            pl.BlockSpec((None, ATTN_KV_HEADS, V_ROWS, tm), lambda b, i: (b, 0, 0, i)),
            pl.BlockSpec((None, tm, CONV_CH), lambda b, i: (b, i, 0)),
        ],
        out_shape=[
            jax.ShapeDtypeStruct((B, ATTN_WIDTH, S), BF16),
            jax.ShapeDtypeStruct((B, S, KV_WIDTH), BF16),
            jax.ShapeDtypeStruct((B, ATTN_KV_HEADS, V_ROWS, S), BF16),
            jax.ShapeDtypeStruct((B, S, CONV_CH), F32),
        ],
        compiler_params=_params("parallel", "parallel"),
        name="l0_in_proj",
    )(x, w, gq, gk, cos, sin, grp)


def _attn_kernel(qT_ref, k_ref, vT_ref, o_ref, *, tk):
    S = k_ref.shape[0]
    tq = qT_ref.shape[1]
    n_k = S // tk
    for kk in range(ATTN_KV_HEADS):
        q_ext = []
        for g in range(ATTN_GROUP):
            h = kk * ATTN_GROUP + g
            qh = qT_ref[h * HEAD_DIM:(h + 1) * HEAD_DIM, :]
            zeros = jnp.zeros_like(qh)
            parts = [zeros] * ATTN_KV_HEADS
            parts[kk] = qh
            q_ext.append(jnp.concatenate(parts, axis=0))

        def body(c, carry):
            start = pl.multiple_of(c * tk, tk)
            k_c = k_ref[pl.ds(start, tk), :]
            v_c = vT_ref[kk, :, pl.ds(start, tk)]
            new = []
            for g in range(ATTN_GROUP):
                m_old, acc = carry[g]
                s = _dot(k_c, q_ext[g])
                m_new = jnp.maximum(m_old, jnp.max(s, axis=0, keepdims=True))
                p = jnp.exp(s - m_new).astype(BF16)
                acc = acc * jnp.exp(m_old - m_new) + _dot(v_c, p)
                new.append((m_new, acc))
            return tuple(new)

        init = tuple((jnp.full((1, tq), -jnp.inf, F32), jnp.zeros((V_ROWS, tq), F32))
                     for _ in range(ATTN_GROUP))
        fin = lax.fori_loop(0, n_k, body, init)
        outs = [acc[:HEAD_DIM] / acc[HEAD_DIM:HEAD_DIM + 1] for _, acc in fin]
        o_t = jnp.concatenate(outs, axis=0)
        w = ATTN_GROUP * HEAD_DIM
        o_ref[:, kk * w:(kk + 1) * w] = o_t.T.astype(o_ref.dtype)


def _attention(qT, k, vT):
    B, _, S = qT.shape
    tq = min(ATTN_TQ, S)
    tk = min(ATTN_TK, S)
    return pl.pallas_call(
        functools.partial(_attn_kernel, tk=tk),
        grid=(B, S // tq),
        in_specs=[
            pl.BlockSpec((None, ATTN_WIDTH, tq), lambda b, i: (b, 0, i)),
            pl.BlockSpec((None, S, KV_WIDTH), lambda b, i: (b, 0, 0)),
            pl.BlockSpec((None, ATTN_KV_HEADS, V_ROWS, S), lambda b, i: (b, 0, 0, 0)),
        ],
        out_specs=pl.BlockSpec((None, tq, ATTN_WIDTH), lambda b, i: (b, i, 0)),
        out_shape=jax.ShapeDtypeStruct((B, S, ATTN_WIDTH), BF16),
        compiler_params=_params("parallel", "arbitrary"),
        name="l0_attention",
    )(qT, k, vT)


def _conv_kernel(zp_ref, zc_ref, zn_ref, w_ref, b_ref, g_ref, beta_ref, o_ref, shift_ref):
    ts = zc_ref.shape[0]
    i = pl.program_id(1)
    n = pl.num_programs(1)
    ext = ts + 2 * CONV_HALO - SUBLANES
    prev = jnp.where(i > 0, zp_ref[...], 0.0)
    nxt = jnp.where(i < n - 1, zn_ref[...], 0.0)
    shift_ref[0, 0:CONV_HALO, :] = prev
    shift_ref[0, CONV_HALO:CONV_HALO + ts, :] = zc_ref[...]
    shift_ref[0, CONV_HALO + ts:, :] = nxt
    for r in range(1, SUBLANES):
        shift_ref[r, 0:ext, :] = shift_ref[0, r:r + ext, :]

    bias = b_ref[...]
    gain = g_ref[...]
    beta = beta_ref[...]
    first = CONV_HALO - CONV_PAD

    def block(rb, carry):
        base = pl.multiple_of(rb * CONV_RB, CONV_RB)
        acc = jnp.zeros((CONV_RB, CONV_CH), F32)
        for j in range(CONV_TAPS):
            off = first + j
            r, a = off % SUBLANES, off // SUBLANES
            rows = shift_ref[r, pl.ds(base + a * SUBLANES, CONV_RB), :]
            acc = acc + rows * w_ref[j:j + 1, :]
        y = _layer_norm(acc + bias, gain, beta)
        o_ref[pl.ds(base, CONV_RB), :] = _swish(y).astype(o_ref.dtype)
        return carry

    lax.fori_loop(0, ts // CONV_RB, block, 0)


def _conformer_conv(z, dw_w, dw_b, g, beta):
    B, S, C = z.shape
    ts = min(ROW_TILE, S)
    nS = S // ts
    hb = ts // CONV_HALO
    last_halo = S // CONV_HALO - 1
    row = lambda v: v.reshape(1, C)
    return pl.pallas_call(
        _conv_kernel,
        grid=(B, nS),
        in_specs=[
            pl.BlockSpec((None, CONV_HALO, C), lambda b, i: (b, jnp.maximum(i * hb - 1, 0), 0)),
            pl.BlockSpec((None, ts, C), lambda b, i: (b, i, 0)),
            pl.BlockSpec((None, CONV_HALO, C), lambda b, i: (b, jnp.minimum((i + 1) * hb, last_halo), 0)),
            _resident((CONV_TAPS, C)),
            _resident((1, C)), _resident((1, C)), _resident((1, C)),
        ],
        out_specs=pl.BlockSpec((None, ts, C), lambda b, i: (b, i, 0)),
        out_shape=jax.ShapeDtypeStruct((B, S, C), BF16),
        scratch_shapes=[pltpu.VMEM((SUBLANES, ts + 2 * CONV_HALO, C), F32)],
        compiler_params=_params("parallel", "parallel"),
        name="l0_conv",
    )(z, z, z, dw_w, row(dw_b), row(g), row(beta))


def _out_proj_kernel(*refs, n_parts):
    part_refs = refs[:n_parts]
    x_ref, w_ref, g_ref, b_ref, o_ref = refs[n_parts:]
    out = None
    row0 = 0
    for p_ref in part_refs:
        width = p_ref.shape[1]
        t = _dot(p_ref[...], w_ref[row0:row0 + width, :])
        out = t if out is None else out + t
        row0 += width
    y = DEEPNORM_ALPHA * x_ref[...] + out
    o_ref[...] = _layer_norm(y, g_ref[...], b_ref[...])


def _out_proj_ln(parts, x, w, g, b):
    M, D = x.shape
    tm = min(ROW_TILE, M)
    row = lambda v: v.reshape(1, D)
    return pl.pallas_call(
        functools.partial(_out_proj_kernel, n_parts=len(parts)),
        grid=(M // tm,),
        in_specs=[pl.BlockSpec((tm, p.shape[1]), lambda i: (i, 0)) for p in parts] + [
            pl.BlockSpec((tm, D), lambda i: (i, 0)),
            _resident(w.shape), _resident((1, D)), _resident((1, D)),
        ],
        out_specs=pl.BlockSpec((tm, D), lambda i: (i, 0)),
        out_shape=jax.ShapeDtypeStruct((M, D), F32),
        compiler_params=_params("parallel"),
        name="out_proj_ln",
    )(*parts, x, w, row(g), row(b))


def _ffn_kernel(x_ref, wg_ref, wu_ref, wd_ref, g_ref, b_ref, o_ref, h_ref):
    x = x_ref[...]
    xb = x.astype(BF16)
    hidden = wg_ref.shape[1]
    for c in range(hidden // FFN_HC):
        cols = slice(c * FFN_HC, (c + 1) * FFN_HC)
        gate = _dot(xb, wg_ref[:, cols])
        up = _dot(xb, wu_ref[:, cols])
        h_ref[:, cols] = (_swish(gate) * up).astype(BF16)
    y = DEEPNORM_ALPHA * x + _dot(h_ref[...], wd_ref[...])
    o_ref[...] = _layer_norm(y, g_ref[...], b_ref[...])


def _ffn_ln(x, wg, wu, wd, g, b):
    M, D = x.shape
    hidden = wg.shape[1]
    assert hidden % FFN_HC == 0
    tm = min(ROW_TILE, M)
    row = lambda v: v.reshape(1, D)
    return pl.pallas_call(
        _ffn_kernel,
        grid=(M // tm,),
        in_specs=[
            pl.BlockSpec((tm, D), lambda i: (i, 0)),
            _resident(wg.shape), _resident(wu.shape), _resident(wd.shape),
            _resident((1, D)), _resident((1, D)),
        ],
        out_specs=pl.BlockSpec((tm, D), lambda i: (i, 0)),
        out_shape=jax.ShapeDtypeStruct((M, D), F32),
        scratch_shapes=[pltpu.VMEM((tm, hidden), BF16)],
        compiler_params=_params("parallel"),
        name="ffn_ln",
    )(x, wg, wu, wd, row(g), row(b))


def _l1_in_kernel(x_ref, w_ref, cos_ref, sin_ref, q_ref, kT_ref, v_ref, sg_ref):
    xb = x_ref[...].astype(BF16)
    cos = cos_ref[...]
    sin = sin_ref[...]
    half = RET_DK // 2

    def rope_head(t):
        t1, t2 = t[:, :half], t[:, half:]
        return jnp.concatenate([t1 * cos - t2 * sin, t1 * sin + t2 * cos], axis=1)

    for h in range(RET_HEADS):
        cols = slice(h * RET_DK, (h + 1) * RET_DK)
        q_ref[:, cols] = rope_head(_dot(xb, w_ref[:, cols])).astype(BF16)
    for h in range(RET_HEADS):
        c0 = RET_QK_WIDTH + h * RET_DK
        kh = rope_head(_dot(xb, w_ref[:, c0:c0 + RET_DK])) * (RET_DK ** -0.5)
        kT_ref[h * RET_DK:(h + 1) * RET_DK, :] = kh.T.astype(BF16)
    for h in range(RET_HEADS):
        c0 = 2 * RET_QK_WIDTH + h * RET_DV
        v_ref[:, h * RET_DV:(h + 1) * RET_DV] = _dot(xb, w_ref[:, c0:c0 + RET_DV]).astype(BF16)
    for h in range(RET_HEADS):
        c0 = 2 * RET_QK_WIDTH + RET_V_WIDTH + h * RET_DV
        sg_ref[:, h * RET_DV:(h + 1) * RET_DV] = _swish(_dot(xb, w_ref[:, c0:c0 + RET_DV])).astype(BF16)


def _l1_in_proj(x, w, cos, sin):
    B, S, D = x.shape
    tm = min(ROW_TILE, S)
    return pl.pallas_call(
        _l1_in_kernel,
        grid=(B, S // tm),
        in_specs=[
            pl.BlockSpec((None, tm, D), lambda b, i: (b, i, 0)),
            _resident(w.shape),
            pl.BlockSpec((tm, LANES), lambda b, i: (i, 0)),
            pl.BlockSpec((tm, LANES), lambda b, i: (i, 0)),
        ],
        out_specs=[
            pl.BlockSpec((None, tm, RET_QK_WIDTH), lambda b, i: (b, i, 0)),
            pl.BlockSpec((None, RET_QK_WIDTH, tm), lambda b, i: (b, 0, i)),
            pl.BlockSpec((None, tm, RET_V_WIDTH), lambda b, i: (b, i, 0)),
            pl.BlockSpec((None, tm, RET_V_WIDTH), lambda b, i: (b, i, 0)),
        ],
        out_shape=[
            jax.ShapeDtypeStruct((B, S, RET_QK_WIDTH), BF16),
            jax.ShapeDtypeStruct((B, RET_QK_WIDTH, S), BF16),
            jax.ShapeDtypeStruct((B, S, RET_V_WIDTH), BF16),
            jax.ShapeDtypeStruct((B, S, RET_V_WIDTH), BF16),
        ],
        compiler_params=_params("parallel", "parallel"),
        name="l1_in_proj",
    )(x, w, cos, sin)


def _ret_bwd_state_kernel(lg_ref, kT_ref, v_ref, sb_ref, state_ref):
    h = pl.program_id(1)
    n_c = sb_ref.shape[0]
    C = RET_CHUNK
    lg = lg_ref[h]

    @pl.when(pl.program_id(2) == 0)
    def _():
        state_ref[...] = jnp.zeros_like(state_ref)

    pos = lax.broadcasted_iota(jnp.int32, (1, C), 1).astype(F32)
    k_decay = jnp.exp(lg * pos)
    chunk_decay = jnp.exp(jnp.full((1, RET_DV), lg * C, F32))
    for c in reversed(range(n_c)):
        tok = slice(c * C, (c + 1) * C)
        sb_ref[c] = state_ref[...].astype(BF16)
        k_dec = (kT_ref[:, tok].astype(F32) * k_decay).astype(BF16)
        state_ref[...] = state_ref[...] * chunk_decay + _dot(k_dec, v_ref[tok, :])


def _ret_main_kernel(lgf_ref, lgb_ref, q_ref, kT_ref, v_ref, sg_ref, sb_ref, gn_ref, y_ref, state_ref):
    h = pl.program_id(1)
    n_c = sb_ref.shape[0]
    C = RET_CHUNK
    lgf = lgf_ref[h]
    lgb = lgb_ref[h]

    @pl.when(pl.program_id(2) == 0)
    def _():
        state_ref[...] = jnp.zeros_like(state_ref)

    row = lax.broadcasted_iota(jnp.int32, (C, C), 0).astype(F32)
    col = lax.broadcasted_iota(jnp.int32, (C, C), 1).astype(F32)
    diff = row - col
    intra = jnp.exp(lgf * jnp.maximum(diff, 0.0) + lgb * jnp.maximum(-diff, 0.0))
    q_decay_f = jnp.exp(lgf * (row + 1.0))
    q_decay_b = jnp.exp(lgb * (C - row))
    pos = lax.broadcasted_iota(jnp.int32, (1, C), 1).astype(F32)
    k_decay_f = jnp.exp(lgf * (C - 1.0 - pos))
    chunk_decay_f = jnp.exp(jnp.full((1, RET_DV), lgf * C, F32))
    gn = gn_ref[...]

    for c in range(n_c):
        tok = slice(c * C, (c + 1) * C)
        qc = q_ref[tok, :]
        kTc = kT_ref[:, tok]
        vc = v_ref[tok, :]
        qf32 = qc.astype(F32)
        p = (_dot(qc, kTc) * intra).astype(BF16)
        o = (_dot(p, vc)
             + _dot((qf32 * q_decay_f).astype(BF16), state_ref[...].astype(BF16))
             + _dot((qf32 * q_decay_b).astype(BF16), sb_ref[c]))
        k_dec = (kTc.astype(F32) * k_decay_f).astype(BF16)
        state_ref[...] = state_ref[...] * chunk_decay_f + _dot(k_dec, vc)
        mu = jnp.mean(o, axis=-1, keepdims=True)
        d = o - mu
        var = jnp.mean(d * d, axis=-1, keepdims=True)
        y = d * lax.rsqrt(var + LN_EPS) * gn
        y_ref[tok, :] = (sg_ref[tok, :].astype(F32) * y).astype(y_ref.dtype)


def _retention(q, kT, v, sg, lgf, lgb, gn):
    B, S, _ = q.shape
    ts = min(RET_STEP, S)
    nS = S // ts
    n_c = ts // RET_CHUNK
    smem = pl.BlockSpec(memory_space=pltpu.SMEM)
    sb = pl.pallas_call(
        _ret_bwd_state_kernel,
        grid=(B, RET_HEADS, nS),
        in_specs=[
            smem,
            pl.BlockSpec((None, RET_DK, ts), lambda b, h, i: (b, h, nS - 1 - i)),
            pl.BlockSpec((None, ts, RET_DV), lambda b, h, i: (b, nS - 1 - i, h)),
        ],
        out_specs=pl.BlockSpec((None, None, n_c, RET_DK, RET_DV), lambda b, h, i: (b, h, nS - 1 - i, 0, 0)),
        out_shape=jax.ShapeDtypeStruct((B, RET_HEADS, S // RET_CHUNK, RET_DK, RET_DV), BF16),
        scratch_shapes=[pltpu.VMEM((RET_DK, RET_DV), F32)],
        compiler_params=_params("parallel", "parallel", "arbitrary"),
        name="l1_ret_bwd_state",
    )(lgb, kT, v)
    return pl.pallas_call(
        _ret_main_kernel,
        grid=(B, RET_HEADS, nS),
        in_specs=[
            smem, smem,
            pl.BlockSpec((None, ts, RET_DK), lambda b, h, i: (b, i, h)),
            pl.BlockSpec((None, RET_DK, ts), lambda b, h, i: (b, h, i)),
            pl.BlockSpec((None, ts, RET_DV), lambda b, h, i: (b, i, h)),
            pl.BlockSpec((None, ts, RET_DV), lambda b, h, i: (b, i, h)),
            pl.BlockSpec((None, None, n_c, RET_DK, RET_DV), lambda b, h, i: (b, h, i, 0, 0)),
            pl.BlockSpec((1, RET_DV), lambda b, h, i: (0, h)),
        ],
        out_specs=pl.BlockSpec((None, ts, RET_DV), lambda b, h, i: (b, i, h)),
        out_shape=jax.ShapeDtypeStruct((B, S, RET_V_WIDTH), BF16),
        scratch_shapes=[pltpu.VMEM((RET_DK, RET_DV), F32)],
        compiler_params=_params("parallel", "parallel", "arbitrary"),
        name="l1_retention",
    )(lgf, lgb, q, kT, v, sg, sb, gn.reshape(1, RET_V_WIDTH))


def _axial_rope(seq_len, head_dim):
    rows = seq_len // GRID_W
    row = jnp.broadcast_to(jnp.arange(rows, dtype=F32)[:, None], (rows, GRID_W)).reshape(seq_len)
    col = jnp.broadcast_to(jnp.arange(GRID_W, dtype=F32)[None, :], (rows, GRID_W)).reshape(seq_len)
    axis_dim = head_dim // 2
    inv_freq = ROPE_THETA ** (-jnp.arange(0, axis_dim, 2, dtype=F32) / axis_dim)
    ang = jnp.concatenate([row[:, None] * inv_freq, col[:, None] * inv_freq], axis=-1)
    return jnp.cos(ang), jnp.sin(ang)


def kernel(x, l0_w_in, l0_q_norm_g, l0_k_norm_g, l0_dw_w, l0_dw_b, l0_conv_norm_g, l0_conv_norm_b,
           l0_w_out, l0_ln_mix_g, l0_ln_mix_b, l0_ffn_w_gate, l0_ffn_w_up, l0_ffn_w_down,
           l0_ln_ffn_g, l0_ln_ffn_b, l1_w_in, l1_log_decay_fwd, l1_log_decay_bwd, l1_ret_norm_g,
           l1_w_out, l1_ln_mix_g, l1_ln_mix_b, l1_ffn_w_gate, l1_ffn_w_up, l1_ffn_w_down,
           l1_ln_ffn_g, l1_ln_ffn_b):
    B, S, D = x.shape
    M = B * S
    bf = lambda w: w.astype(BF16)

    cos_a, sin_a = _axial_rope(S, HEAD_DIM)
    cos_l = jnp.tile(jnp.concatenate([cos_a, cos_a], axis=-1), (1, LANES // HEAD_DIM))
    sin_l = jnp.tile(jnp.concatenate([-sin_a, sin_a], axis=-1), (1, LANES // HEAD_DIM))
    cos_r, sin_r = _axial_rope(S, RET_DK)
    head_id = jnp.arange(ATTN_WIDTH) // HEAD_DIM
    grp = (head_id[:, None] == head_id[None, :]).astype(BF16)

    qT, k, vT, z = _l0_in_proj(
        x, bf(l0_w_in),
        jnp.tile(l0_q_norm_g, ATTN_HEADS).reshape(1, ATTN_WIDTH),
        jnp.tile(l0_k_norm_g, ATTN_KV_HEADS).reshape(1, KV_WIDTH),
        cos_l, sin_l, grp)
    attn = _attention(qT, k, vT)
    conv = _conformer_conv(z, l0_dw_w, l0_dw_b, l0_conv_norm_g, l0_conv_norm_b)
    x2d = x.reshape(M, D)
    x2d = _out_proj_ln([attn.reshape(M, ATTN_WIDTH), conv.reshape(M, CONV_CH)], x2d, bf(l0_w_out),
                       l0_ln_mix_g, l0_ln_mix_b)
    x2d = _ffn_ln(x2d, bf(l0_ffn_w_gate), bf(l0_ffn_w_up), bf(l0_ffn_w_down), l0_ln_ffn_g, l0_ln_ffn_b)

    q, kT, v, sg = _l1_in_proj(x2d.reshape(B, S, D), bf(l1_w_in), cos_r, sin_r)
    y = _retention(q, kT, v, sg, l1_log_decay_fwd, l1_log_decay_bwd, l1_ret_norm_g)
    x2d = _out_proj_ln([y.reshape(M, RET_V_WIDTH)], x2d, bf(l1_w_out), l1_ln_mix_g, l1_ln_mix_b)
    x2d = _ffn_ln(x2d, bf(l1_ffn_w_gate), bf(l1_ffn_w_up), bf(l1_ffn_w_down), l1_ln_ffn_g, l1_ln_ffn_b)
    return x2d.reshape(B, S, D)
```

```python
import functools

import jax
import jax.numpy as jnp
from jax import lax
from jax.experimental import pallas as pl
from jax.experimental.pallas import tpu as pltpu

F32 = jnp.float32
BF16 = jnp.bfloat16

GRID_W = 64
ROPE_THETA = 10000.0
ATTN_HEADS = 8
ATTN_KV_HEADS = 2
ATTN_GROUP = ATTN_HEADS // ATTN_KV_HEADS
HEAD_DIM = 64
ATTN_WIDTH = ATTN_HEADS * HEAD_DIM
KV_WIDTH = ATTN_KV_HEADS * HEAD_DIM
CONV_CH = 512
CONV_TAPS = 31
CONV_PAD = CONV_TAPS // 2
RET_HEADS = 4
RET_DK = 256
RET_DV = 512
RET_QK_WIDTH = RET_HEADS * RET_DK
RET_V_WIDTH = RET_HEADS * RET_DV
DEPTH = 2
DEEPNORM_ALPHA = (2 * DEPTH) ** 0.25
LN_EPS = 1e-5
RMS_EPS = 1e-6
LOG2_E = 1.4426950408889634

LANES = 128
SUBLANES = 8
BF16_ROWS = 16
VMEM_LIMIT_BYTES = 56 * 1024 * 1024

ROW_TILE = 512
ATTN_TQ = 256
ATTN_TK = 512
ATTN_HEADS_PER_BODY = 1
ATTN_SLOTS = 4
ATTN_UNROLL = 8
V_ROWS = HEAD_DIM + BF16_ROWS
CONV_HALO = 16
CONV_RB = 32
RET_CHUNK = 256
RET_STEP = 2048
FFN_HC = 256


def _dot(a, b):
    return jnp.dot(a, b, preferred_element_type=F32)


def _layer_norm(y, g, b):
    mu = jnp.mean(y, axis=-1, keepdims=True)
    d = y - mu
    var = jnp.mean(d * d, axis=-1, keepdims=True)
    return d * lax.rsqrt(var + LN_EPS) * g + b


def _swish(t):
    return t * jax.nn.sigmoid(t)


def _params(*semantics):
    return pltpu.CompilerParams(dimension_semantics=semantics, vmem_limit_bytes=VMEM_LIMIT_BYTES)


def _resident(shape):
    nd = len(shape)
    return pl.BlockSpec(shape, lambda *_: (0,) * nd, pipeline_mode=pl.Buffered(1))


def _l0_in_kernel(x_ref, w_ref, gq_ref, gk_ref, cos_ref, sin_ref, grp_ref,
                  qT_ref, k_ref, vT_ref, z_ref):
    tm = x_ref.shape[0]
    xb = x_ref[...].astype(BF16)
    cos = cos_ref[...]
    sin = sin_ref[...]
    lane = lax.broadcasted_iota(jnp.int32, (tm, LANES), 1)
    first_half = (lane % HEAD_DIM) < (HEAD_DIM // 2)

    def norm_rope(t, g, grp):
        t2 = t * t
        hi = t2.astype(BF16)
        lo = (t2 - hi.astype(F32)).astype(BF16)
        ss = _dot(hi, grp) + _dot(lo, grp)
        tn = t * lax.rsqrt(ss * (1.0 / HEAD_DIM) + RMS_EPS) * g
        outs = []
        for j in range(t.shape[1] // LANES):
            c = tn[:, j * LANES:(j + 1) * LANES]
            partner = jnp.where(first_half,
                                pltpu.roll(c, LANES - HEAD_DIM // 2, 1),
                                pltpu.roll(c, HEAD_DIM // 2, 1))
            outs.append(c * cos + partner * sin)
        return outs[0] if len(outs) == 1 else jnp.concatenate(outs, axis=1)

    c0 = 0
    q = _dot(xb, w_ref[:, c0:c0 + ATTN_WIDTH]); c0 += ATTN_WIDTH
    k = _dot(xb, w_ref[:, c0:c0 + KV_WIDTH]); c0 += KV_WIDTH
    v = _dot(xb, w_ref[:, c0:c0 + KV_WIDTH]); c0 += KV_WIDTH
    a = _dot(xb, w_ref[:, c0:c0 + CONV_CH]); c0 += CONV_CH
    gate = _dot(xb, w_ref[:, c0:c0 + CONV_CH])

    qr = norm_rope(q, gq_ref[...], grp_ref[...]) * (HEAD_DIM ** -0.5 * LOG2_E)
    qT_ref[...] = qr.T.astype(BF16)
    k_ref[...] = norm_rope(k, gk_ref[...], grp_ref[:KV_WIDTH, :KV_WIDTH]).astype(BF16)
    vT = v.T
    ones = jnp.ones((BF16_ROWS, tm), F32)
    for kk in range(ATTN_KV_HEADS):
        vT_ref[kk] = jnp.concatenate(
            [vT[kk * HEAD_DIM:(kk + 1) * HEAD_DIM], ones], axis=0).astype(BF16)
    z_ref[...] = a * jax.nn.sigmoid(gate)


def _l0_in_proj(x, w, gq, gk, cos, sin, grp):
    B, S, D = x.shape
    tm = min(ROW_TILE, S)
    nS = S // tm
    n_in = w.shape[1]
    return pl.pallas_call(
        _l0_in_kernel,
        grid=(B, nS),
        in_specs=[
            pl.BlockSpec((None, tm, D), lambda b, i: (b, i, 0)),
            _resident((D, n_in)),
            _resident((1, ATTN_WIDTH)),
            _resident((1, KV_WIDTH)),
            pl.BlockSpec((tm, LANES), lambda b, i: (i, 0)),
            pl.BlockSpec((tm, LANES), lambda b, i: (i, 0)),
            _resident((ATTN_WIDTH, ATTN_WIDTH)),
        ],
        out_specs=[
            pl.BlockSpec((None, ATTN_WIDTH, tm), lambda b, i: (b, 0, i)),
            pl.BlockSpec((None, tm, KV_WIDTH), lambda b, i: (b, i, 0)),
            pl.BlockSpec((None, ATTN_KV_HEADS, V_ROWS, tm), lambda b, i: (b, 0, 0, i)),
            pl.BlockSpec((None, tm, CONV_CH), lambda b, i: (b, i, 0)),
        ],
        out_shape=[
            jax.ShapeDtypeStruct((B, ATTN_WIDTH, S), BF16),
            jax.ShapeDtypeStruct((B, S, KV_WIDTH), BF16),
            jax.ShapeDtypeStruct((B, ATTN_KV_HEADS, V_ROWS, S), BF16),
            jax.ShapeDtypeStruct((B, S, CONV_CH), F32),
        ],
        compiler_params=_params("parallel", "parallel"),
        name="l0_in_proj",
    )(x, w, gq, gk, cos, sin, grp)


def _attn_kernel(qT_ref, k_ref, vT_ref, o_ref, s_buf, p_buf, oT_ref, *, tk):
    S = k_ref.shape[0]
    tq = qT_ref.shape[1]
    n_k = S // tk
    U = ATTN_SLOTS
    LA = U - 1
    unroll = ATTN_UNROLL
    assert n_k > LA and unroll % U == 0
    row_head = lax.broadcasted_iota(jnp.int32, (KV_WIDTH, tq), 0) // HEAD_DIM
    nh = ATTN_HEADS_PER_BODY
    heads = range(nh)

    def head(hg, carry_unused):
        kk = (hg * nh) // ATTN_GROUP
        q_ext = []
        for j in heads:
            row0 = pl.multiple_of((hg * nh + j) * HEAD_DIM, HEAD_DIM)
            qh = qT_ref[pl.ds(row0, HEAD_DIM), :]
            q_ext.append(jnp.where(row_head == kk, jnp.concatenate([qh] * ATTN_KV_HEADS, axis=0), 0))

        def scores(c, slot):
            start = pl.multiple_of(c * tk, tk)
            k_c = k_ref[pl.ds(start, tk), :]
            out = []
            for j in heads:
                s = _dot(k_c, q_ext[j])
                s_buf[j, slot] = s
                out.append(jnp.max(s, axis=0, keepdims=True))
            return out

        def softmax(slot, m_run, m_chunk):
            m_new = [jnp.maximum(a, b) for a, b in zip(m_run, m_chunk)]
            for j in heads:
                p_buf[j, slot] = jnp.exp2((s_buf[j, slot] - m_new[j]).astype(BF16))
            return m_new, [jnp.exp2(a - b) for a, b in zip(m_run, m_new)]

        def values(c, slot, acc, alpha):
            start = pl.multiple_of(c * tk, tk)
            v_c = vT_ref[kk, :, pl.ds(start, tk)]
            return [acc[j] * alpha[j] + _dot(v_c, p_buf[j, slot]) for j in heads]

        def substep(tau, r, carry, do_values=True, do_scores=True):
            m_run, m_chunks, alpha, acc = carry
            m_chunks = list(m_chunks)
            if do_values:
                acc = values(tau - 1, (r - 1) % U, acc, alpha)
            if do_scores:
                m_chunks[(r + LA) % U] = scores(tau + LA, (r + LA) % U)
            m_run, alpha = softmax(r, m_run, m_chunks[r])
            return m_run, tuple(m_chunks), alpha, acc

        neg_inf = [jnp.full((1, tq), -jnp.inf, F32) for _ in heads]
        m_chunks = [neg_inf] * U
        for c in range(LA):
            m_chunks[c] = scores(c, c)
        carry = (neg_inf, tuple(m_chunks), neg_inf, [jnp.zeros((V_ROWS, tq), F32) for _ in heads])
        carry = substep(0, 0, carry, do_values=False)

        def body(i, carry):
            tau0 = 1 + unroll * i
            for u in range(unroll):
                carry = substep(tau0 + u, (1 + u) % U, carry)
            return carry

        n_body = (n_k - LA - 1) // unroll
        carry = lax.fori_loop(0, n_body, body, carry)
        for tau in range(1 + unroll * n_body, n_k):
            carry = substep(tau, tau % U, carry, do_scores=tau + LA < n_k)
        _, _, alpha, acc = carry
        acc = values(n_k - 1, (n_k - 1) % U, acc, alpha)
        for j in heads:
            row0 = pl.multiple_of((hg * nh + j) * HEAD_DIM, HEAD_DIM)
            oT_ref[pl.ds(row0, HEAD_DIM), :] = acc[j][:HEAD_DIM] / acc[j][HEAD_DIM:HEAD_DIM + 1]
        return carry_unused

    lax.fori_loop(0, ATTN_HEADS // nh, head, 0)
    o_ref[...] = oT_ref[...].T.astype(o_ref.dtype)


def _attention(qT, k, vT):
    B, _, S = qT.shape
    tq = min(ATTN_TQ, S)
    tk = min(ATTN_TK, S // 4)
    return pl.pallas_call(
        functools.partial(_attn_kernel, tk=tk),
        grid=(B, S // tq),
        in_specs=[
            pl.BlockSpec((None, ATTN_WIDTH, tq), lambda b, i: (b, 0, i)),
            pl.BlockSpec((None, S, KV_WIDTH), lambda b, i: (b, 0, 0)),
            pl.BlockSpec((None, ATTN_KV_HEADS, V_ROWS, S), lambda b, i: (b, 0, 0, 0)),
        ],
        out_specs=pl.BlockSpec((None, tq, ATTN_WIDTH), lambda b, i: (b, i, 0)),
        out_shape=jax.ShapeDtypeStruct((B, S, ATTN_WIDTH), BF16),
        scratch_shapes=[
            pltpu.VMEM((ATTN_HEADS_PER_BODY, ATTN_SLOTS, tk, tq), F32),
            pltpu.VMEM((ATTN_HEADS_PER_BODY, ATTN_SLOTS, tk, tq), BF16),
            pltpu.VMEM((ATTN_WIDTH, tq), F32),
        ],
        compiler_params=_params("parallel", "arbitrary"),
        name="l0_attention",
    )(qT, k, vT)


def _conv_kernel(zp_ref, zc_ref, zn_ref, w_ref, b_ref, g_ref, beta_ref, o_ref, shift_ref):
    ts = zc_ref.shape[0]
    i = pl.program_id(1)
    n = pl.num_programs(1)
    ext = ts + 2 * CONV_HALO - SUBLANES
    prev = jnp.where(i > 0, zp_ref[...], 0.0)
    nxt = jnp.where(i < n - 1, zn_ref[...], 0.0)
    shift_ref[0, 0:CONV_HALO, :] = prev
    shift_ref[0, CONV_HALO:CONV_HALO + ts, :] = zc_ref[...]
    shift_ref[0, CONV_HALO + ts:, :] = nxt
    for r in range(1, SUBLANES):
        shift_ref[r, 0:ext, :] = shift_ref[0, r:r + ext, :]

    bias = b_ref[...]
    gain = g_ref[...]
    beta = beta_ref[...]
    first = CONV_HALO - CONV_PAD

    def block(rb, carry):
        base = pl.multiple_of(rb * CONV_RB, CONV_RB)
        acc = jnp.zeros((CONV_RB, CONV_CH), F32)
        for j in range(CONV_TAPS):
            off = first + j
            r, a = off % SUBLANES, off // SUBLANES
            rows = shift_ref[r, pl.ds(base + a * SUBLANES, CONV_RB), :]
            acc = acc + rows * w_ref[j:j + 1, :]
        y = _layer_norm(acc + bias, gain, beta)
        o_ref[pl.ds(base, CONV_RB), :] = _swish(y).astype(o_ref.dtype)
        return carry

    lax.fori_loop(0, ts // CONV_RB, block, 0)


def _conformer_conv(z, dw_w, dw_b, g, beta):
    B, S, C = z.shape
    ts = min(ROW_TILE, S)
    nS = S // ts
    hb = ts // CONV_HALO
    last_halo = S // CONV_HALO - 1
    row = lambda v: v.reshape(1, C)
    return pl.pallas_call(
        _conv_kernel,
        grid=(B, nS),
        in_specs=[
            pl.BlockSpec((None, CONV_HALO, C), lambda b, i: (b, jnp.maximum(i * hb - 1, 0), 0)),
            pl.BlockSpec((None, ts, C), lambda b, i: (b, i, 0)),
            pl.BlockSpec((None, CONV_HALO, C), lambda b, i: (b, jnp.minimum((i + 1) * hb, last_halo), 0)),
            _resident((CONV_TAPS, C)),
            _resident((1, C)), _resident((1, C)), _resident((1, C)),
        ],
        out_specs=pl.BlockSpec((None, ts, C), lambda b, i: (b, i, 0)),
        out_shape=jax.ShapeDtypeStruct((B, S, C), BF16),
        scratch_shapes=[pltpu.VMEM((SUBLANES, ts + 2 * CONV_HALO, C), F32)],
        compiler_params=_params("parallel", "parallel"),
        name="l0_conv",
    )(z, z, z, dw_w, row(dw_b), row(g), row(beta))


def _out_proj_kernel(*refs, n_parts):
    part_refs = refs[:n_parts]
    x_ref, w_ref, g_ref, b_ref, o_ref = refs[n_parts:]
    out = None
    row0 = 0
    for p_ref in part_refs:
        width = p_ref.shape[1]
        t = _dot(p_ref[...], w_ref[row0:row0 + width, :])
        out = t if out is None else out + t
        row0 += width
    y = DEEPNORM_ALPHA * x_ref[...] + out
    o_ref[...] = _layer_norm(y, g_ref[...], b_ref[...])


def _out_proj_ln(parts, x, w, g, b):
    M, D = x.shape
    tm = min(ROW_TILE, M)
    row = lambda v: v.reshape(1, D)
    return pl.pallas_call(
        functools.partial(_out_proj_kernel, n_parts=len(parts)),
        grid=(M // tm,),
        in_specs=[pl.BlockSpec((tm, p.shape[1]), lambda i: (i, 0)) for p in parts] + [
            pl.BlockSpec((tm, D), lambda i: (i, 0)),
            _resident(w.shape), _resident((1, D)), _resident((1, D)),
        ],
        out_specs=pl.BlockSpec((tm, D), lambda i: (i, 0)),
        out_shape=jax.ShapeDtypeStruct((M, D), F32),
        compiler_params=_params("parallel"),
        name="out_proj_ln",
    )(*parts, x, w, row(g), row(b))


def _ffn_kernel(x_ref, wg_ref, wu_ref, wd_ref, g_ref, b_ref, o_ref, h_ref):
    x = x_ref[...]
    xb = x.astype(BF16)
    hidden = wg_ref.shape[1]
    for c in range(hidden // FFN_HC):
        cols = slice(c * FFN_HC, (c + 1) * FFN_HC)
        gate = _dot(xb, wg_ref[:, cols])
        up = _dot(xb, wu_ref[:, cols])
        h_ref[:, cols] = (_swish(gate) * up).astype(BF16)
    y = DEEPNORM_ALPHA * x + _dot(h_ref[...], wd_ref[...])
    o_ref[...] = _layer_norm(y, g_ref[...], b_ref[...])


def _ffn_ln(x, wg, wu, wd, g, b):
    M, D = x.shape
    hidden = wg.shape[1]
    assert hidden % FFN_HC == 0
    tm = min(ROW_TILE, M)
    row = lambda v: v.reshape(1, D)
    return pl.pallas_call(
        _ffn_kernel,
        grid=(M // tm,),
        in_specs=[
            pl.BlockSpec((tm, D), lambda i: (i, 0)),
            _resident(wg.shape), _resident(wu.shape), _resident(wd.shape),
            _resident((1, D)), _resident((1, D)),
        ],
        out_specs=pl.BlockSpec((tm, D), lambda i: (i, 0)),
        out_shape=jax.ShapeDtypeStruct((M, D), F32),
        scratch_shapes=[pltpu.VMEM((tm, hidden), BF16)],
        compiler_params=_params("parallel"),
        name="ffn_ln",
    )(x, wg, wu, wd, row(g), row(b))


def _l1_in_kernel(x_ref, w_ref, cos_ref, sin_ref, q_ref, kT_ref, v_ref, sg_ref):
    xb = x_ref[...].astype(BF16)
    cos = cos_ref[...]
    sin = sin_ref[...]
    half = RET_DK // 2

    def rope_head(t):
        t1, t2 = t[:, :half], t[:, half:]
        return jnp.concatenate([t1 * cos - t2 * sin, t1 * sin + t2 * cos], axis=1)

    for h in range(RET_HEADS):
        cols = slice(h * RET_DK, (h + 1) * RET_DK)
        q_ref[:, cols] = rope_head(_dot(xb, w_ref[:, cols])).astype(BF16)
    for h in range(RET_HEADS):
        c0 = RET_QK_WIDTH + h * RET_DK
        kh = rope_head(_dot(xb, w_ref[:, c0:c0 + RET_DK])) * (RET_DK ** -0.5)
        kT_ref[h * RET_DK:(h + 1) * RET_DK, :] = kh.T.astype(BF16)
    for h in range(RET_HEADS):
        c0 = 2 * RET_QK_WIDTH + h * RET_DV
        v_ref[:, h * RET_DV:(h + 1) * RET_DV] = _dot(xb, w_ref[:, c0:c0 + RET_DV]).astype(BF16)
    for h in range(RET_HEADS):
        c0 = 2 * RET_QK_WIDTH + RET_V_WIDTH + h * RET_DV
        sg_ref[:, h * RET_DV:(h + 1) * RET_DV] = _swish(_dot(xb, w_ref[:, c0:c0 + RET_DV])).astype(BF16)


def _l1_in_proj(x, w, cos, sin):
    B, S, D = x.shape
    tm = min(ROW_TILE, S)
    return pl.pallas_call(
        _l1_in_kernel,
        grid=(B, S // tm),
        in_specs=[
            pl.BlockSpec((None, tm, D), lambda b, i: (b, i, 0)),
            _resident(w.shape),
            pl.BlockSpec((tm, LANES), lambda b, i: (i, 0)),
            pl.BlockSpec((tm, LANES), lambda b, i: (i, 0)),
        ],
        out_specs=[
            pl.BlockSpec((None, tm, RET_QK_WIDTH), lambda b, i: (b, i, 0)),
            pl.BlockSpec((None, RET_QK_WIDTH, tm), lambda b, i: (b, 0, i)),
            pl.BlockSpec((None, tm, RET_V_WIDTH), lambda b, i: (b, i, 0)),
            pl.BlockSpec((None, tm, RET_V_WIDTH), lambda b, i: (b, i, 0)),
        ],
        out_shape=[
            jax.ShapeDtypeStruct((B, S, RET_QK_WIDTH), BF16),
            jax.ShapeDtypeStruct((B, RET_QK_WIDTH, S), BF16),
            jax.ShapeDtypeStruct((B, S, RET_V_WIDTH), BF16),
            jax.ShapeDtypeStruct((B, S, RET_V_WIDTH), BF16),
        ],
        compiler_params=_params("parallel", "parallel"),
        name="l1_in_proj",
    )(x, w, cos, sin)


def _ret_bwd_state_kernel(lg_ref, kT_ref, v_ref, sb_ref, state_ref):
    h = pl.program_id(1)
    n_c = sb_ref.shape[0]
    C = RET_CHUNK
    lg = lg_ref[h]

    @pl.when(pl.program_id(2) == 0)
    def _():
        state_ref[...] = jnp.zeros_like(state_ref)

    pos = lax.broadcasted_iota(jnp.int32, (1, C), 1).astype(F32)
    k_decay = jnp.exp(lg * pos)
    chunk_decay = jnp.exp(jnp.full((1, RET_DV), lg * C, F32))
    for c in reversed(range(n_c)):
        tok = slice(c * C, (c + 1) * C)
        sb_ref[c] = state_ref[...].astype(BF16)
        k_dec = (kT_ref[:, tok].astype(F32) * k_decay).astype(BF16)
        state_ref[...] = state_ref[...] * chunk_decay + _dot(k_dec, v_ref[tok, :])


def _ret_main_kernel(lgf_ref, lgb_ref, q_ref, kT_ref, v_ref, sg_ref, sb_ref, gn_ref, y_ref, state_ref):
    h = pl.program_id(1)
    n_c = sb_ref.shape[0]
    C = RET_CHUNK
    lgf = lgf_ref[h]
    lgb = lgb_ref[h]

    @pl.when(pl.program_id(2) == 0)
    def _():
        state_ref[...] = jnp.zeros_like(state_ref)

    row = lax.broadcasted_iota(jnp.int32, (C, C), 0).astype(F32)
    col = lax.broadcasted_iota(jnp.int32, (C, C), 1).astype(F32)
    diff = row - col
    intra = jnp.exp(lgf * jnp.maximum(diff, 0.0) + lgb * jnp.maximum(-diff, 0.0))
    q_decay_f = jnp.exp(lgf * (row + 1.0))
    q_decay_b = jnp.exp(lgb * (C - row))
    pos = lax.broadcasted_iota(jnp.int32, (1, C), 1).astype(F32)
    k_decay_f = jnp.exp(lgf * (C - 1.0 - pos))
    chunk_decay_f = jnp.exp(jnp.full((1, RET_DV), lgf * C, F32))
    gn = gn_ref[...]

    for c in range(n_c):
        tok = slice(c * C, (c + 1) * C)
        qc = q_ref[tok, :]
        kTc = kT_ref[:, tok]
        vc = v_ref[tok, :]
        qf32 = qc.astype(F32)
        p = (_dot(qc, kTc) * intra).astype(BF16)
        o = (_dot(p, vc)
             + _dot((qf32 * q_decay_f).astype(BF16), state_ref[...].astype(BF16))
             + _dot((qf32 * q_decay_b).astype(BF16), sb_ref[c]))
        k_dec = (kTc.astype(F32) * k_decay_f).astype(BF16)
        state_ref[...] = state_ref[...] * chunk_decay_f + _dot(k_dec, vc)
        mu = jnp.mean(o, axis=-1, keepdims=True)
        d = o - mu
        var = jnp.mean(d * d, axis=-1, keepdims=True)
        y = d * lax.rsqrt(var + LN_EPS) * gn
        y_ref[tok, :] = (sg_ref[tok, :].astype(F32) * y).astype(y_ref.dtype)


def _retention(q, kT, v, sg, lgf, lgb, gn):
    B, S, _ = q.shape
    ts = min(RET_STEP, S)
    nS = S // ts
    n_c = ts // RET_CHUNK
    smem = pl.BlockSpec(memory_space=pltpu.SMEM)
    sb = pl.pallas_call(
        _ret_bwd_state_kernel,
        grid=(B, RET_HEADS, nS),
        in_specs=[
            smem,
            pl.BlockSpec((None, RET_DK, ts), lambda b, h, i: (b, h, nS - 1 - i)),
            pl.BlockSpec((None, ts, RET_DV), lambda b, h, i: (b, nS - 1 - i, h)),
        ],
        out_specs=pl.BlockSpec((None, None, n_c, RET_DK, RET_DV), lambda b, h, i: (b, h, nS - 1 - i, 0, 0)),
        out_shape=jax.ShapeDtypeStruct((B, RET_HEADS, S // RET_CHUNK, RET_DK, RET_DV), BF16),
        scratch_shapes=[pltpu.VMEM((RET_DK, RET_DV), F32)],
        compiler_params=_params("parallel", "parallel", "arbitrary"),
        name="l1_ret_bwd_state",
    )(lgb, kT, v)
    return pl.pallas_call(
        _ret_main_kernel,
        grid=(B, RET_HEADS, nS),
        in_specs=[
            smem, smem,
            pl.BlockSpec((None, ts, RET_DK), lambda b, h, i: (b, i, h)),
            pl.BlockSpec((None, RET_DK, ts), lambda b, h, i: (b, h, i)),
            pl.BlockSpec((None, ts, RET_DV), lambda b, h, i: (b, i, h)),
            pl.BlockSpec((None, ts, RET_DV), lambda b, h, i: (b, i, h)),
            pl.BlockSpec((None, None, n_c, RET_DK, RET_DV), lambda b, h, i: (b, h, i, 0, 0)),
            pl.BlockSpec((1, RET_DV), lambda b, h, i: (0, h)),
        ],
        out_specs=pl.BlockSpec((None, ts, RET_DV), lambda b, h, i: (b, i, h)),
        out_shape=jax.ShapeDtypeStruct((B, S, RET_V_WIDTH), BF16),
        scratch_shapes=[pltpu.VMEM((RET_DK, RET_DV), F32)],
        compiler_params=_params("parallel", "parallel", "arbitrary"),
        name="l1_retention",
    )(lgf, lgb, q, kT, v, sg, sb, gn.reshape(1, RET_V_WIDTH))


def _axial_rope(seq_len, head_dim):
    rows = seq_len // GRID_W
    row = jnp.broadcast_to(jnp.arange(rows, dtype=F32)[:, None], (rows, GRID_W)).reshape(seq_len)
    col = jnp.broadcast_to(jnp.arange(GRID_W, dtype=F32)[None, :], (rows, GRID_W)).reshape(seq_len)
    axis_dim = head_dim // 2
    inv_freq = ROPE_THETA ** (-jnp.arange(0, axis_dim, 2, dtype=F32) / axis_dim)
    ang = jnp.concatenate([row[:, None] * inv_freq, col[:, None] * inv_freq], axis=-1)
    return jnp.cos(ang), jnp.sin(ang)


def kernel(x, l0_w_in, l0_q_norm_g, l0_k_norm_g, l0_dw_w, l0_dw_b, l0_conv_norm_g, l0_conv_norm_b,
           l0_w_out, l0_ln_mix_g, l0_ln_mix_b, l0_ffn_w_gate, l0_ffn_w_up, l0_ffn_w_down,
           l0_ln_ffn_g, l0_ln_ffn_b, l1_w_in, l1_log_decay_fwd, l1_log_decay_bwd, l1_ret_norm_g,
           l1_w_out, l1_ln_mix_g, l1_ln_mix_b, l1_ffn_w_gate, l1_ffn_w_up, l1_ffn_w_down,
           l1_ln_ffn_g, l1_ln_ffn_b):
    B, S, D = x.shape
    M = B * S
    bf = lambda w: w.astype(BF16)

    cos_a, sin_a = _axial_rope(S, HEAD_DIM)
    cos_l = jnp.tile(jnp.concatenate([cos_a, cos_a], axis=-1), (1, LANES // HEAD_DIM))
    sin_l = jnp.tile(jnp.concatenate([-sin_a, sin_a], axis=-1), (1, LANES // HEAD_DIM))
    cos_r, sin_r = _axial_rope(S, RET_DK)
    head_id = jnp.arange(ATTN_WIDTH) // HEAD_DIM
    grp = (head_id[:, None] == head_id[None, :]).astype(BF16)

    qT, k, vT, z = _l0_in_proj(
        x, bf(l0_w_in),
        jnp.tile(l0_q_norm_g, ATTN_HEADS).reshape(1, ATTN_WIDTH),
        jnp.tile(l0_k_norm_g, ATTN_KV_HEADS).reshape(1, KV_WIDTH),
        cos_l, sin_l, grp)
    attn = _attention(qT, k, vT)
    conv = _conformer_conv(z, l0_dw_w, l0_dw_b, l0_conv_norm_g, l0_conv_norm_b)
    x2d = x.reshape(M, D)
    x2d = _out_proj_ln([attn.reshape(M, ATTN_WIDTH), conv.reshape(M, CONV_CH)], x2d, bf(l0_w_out),
                       l0_ln_mix_g, l0_ln_mix_b)
    x2d = _ffn_ln(x2d, bf(l0_ffn_w_gate), bf(l0_ffn_w_up), bf(l0_ffn_w_down), l0_ln_ffn_g, l0_ln_ffn_b)

    q, kT, v, sg = _l1_in_proj(x2d.reshape(B, S, D), bf(l1_w_in), cos_r, sin_r)
    y = _retention(q, kT, v, sg, l1_log_decay_fwd, l1_log_decay_bwd, l1_ret_norm_g)
    x2d = _out_proj_ln([y.reshape(M, RET_V_WIDTH)], x2d, bf(l1_w_out), l1_ln_mix_g, l1_ln_mix_b)
    x2d = _ffn_ln(x2d, bf(l1_ffn_w_gate), bf(l1_ffn_w_up), bf(l1_ffn_w_down), l1_ln_ffn_g, l1_ln_ffn_b)
    return x2d.reshape(B, S, D)
```

```python
import functools

import jax
import jax.numpy as jnp
from jax import lax
from jax.experimental import pallas as pl
from jax.experimental.pallas import tpu as pltpu

F32 = jnp.float32
BF16 = jnp.bfloat16

GRID_W = 64
ROPE_THETA = 10000.0
ATTN_HEADS = 8
ATTN_KV_HEADS = 2
ATTN_GROUP = ATTN_HEADS // ATTN_KV_HEADS
HEAD_DIM = 64
ATTN_WIDTH = ATTN_HEADS * HEAD_DIM
KV_WIDTH = ATTN_KV_HEADS * HEAD_DIM
CONV_CH = 512
CONV_TAPS = 31
CONV_PAD = CONV_TAPS // 2
RET_HEADS = 4
RET_DK = 256
RET_DV = 512
RET_QK_WIDTH = RET_HEADS * RET_DK
RET_V_WIDTH = RET_HEADS * RET_DV
DEPTH = 2
DEEPNORM_ALPHA = (2 * DEPTH) ** 0.25
LN_EPS = 1e-5
RMS_EPS = 1e-6
LOG2_E = 1.4426950408889634

LANES = 128
SUBLANES = 8
BF16_ROWS = 16
VMEM_LIMIT_BYTES = 56 * 1024 * 1024

ROW_TILE = 512
ATTN_TQ = 256
ATTN_TK = 512
ATTN_SLOTS = 4
ATTN_UNROLL = 16
V_ROWS = HEAD_DIM + BF16_ROWS
CONV_HALO = 16
CONV_RB = 32
RET_CHUNK = 256
RET_STEP = 2048
FFN_HC = 256


def _dot(a, b):
    return jnp.dot(a, b, preferred_element_type=F32)


def _layer_norm(y, g, b):
    mu = jnp.mean(y, axis=-1, keepdims=True)
    d = y - mu
    var = jnp.mean(d * d, axis=-1, keepdims=True)
    return d * lax.rsqrt(var + LN_EPS) * g + b


def _swish(t):
    return t * jax.nn.sigmoid(t)


def _params(*semantics, flags=None):
    return pltpu.CompilerParams(dimension_semantics=semantics, vmem_limit_bytes=VMEM_LIMIT_BYTES, flags=flags)


def _resident(shape):
    nd = len(shape)
    return pl.BlockSpec(shape, lambda *_: (0,) * nd, pipeline_mode=pl.Buffered(1))


def _l0_in_kernel(x_ref, w_ref, gq_ref, gk_ref, cos_ref, sin_ref, grp_ref,
                  qT_ref, k_ref, vT_ref, z_ref):
    tm = x_ref.shape[0]
    xb = x_ref[...].astype(BF16)
    cos = cos_ref[...]
    sin = sin_ref[...]
    lane = lax.broadcasted_iota(jnp.int32, (tm, LANES), 1)
    first_half = (lane % HEAD_DIM) < (HEAD_DIM // 2)

    def norm_rope(t, g, grp):
        t2 = t * t
        hi = t2.astype(BF16)
        lo = (t2 - hi.astype(F32)).astype(BF16)
        ss = _dot(hi, grp) + _dot(lo, grp)
        tn = t * lax.rsqrt(ss * (1.0 / HEAD_DIM) + RMS_EPS) * g
        outs = []
        for j in range(t.shape[1] // LANES):
            c = tn[:, j * LANES:(j + 1) * LANES]
            partner = jnp.where(first_half,
                                pltpu.roll(c, LANES - HEAD_DIM // 2, 1),
                                pltpu.roll(c, HEAD_DIM // 2, 1))
            outs.append(c * cos + partner * sin)
        return outs[0] if len(outs) == 1 else jnp.concatenate(outs, axis=1)

    c0 = 0
    q = _dot(xb, w_ref[:, c0:c0 + ATTN_WIDTH]); c0 += ATTN_WIDTH
    k = _dot(xb, w_ref[:, c0:c0 + KV_WIDTH]); c0 += KV_WIDTH
    v = _dot(xb, w_ref[:, c0:c0 + KV_WIDTH]); c0 += KV_WIDTH
    a = _dot(xb, w_ref[:, c0:c0 + CONV_CH]); c0 += CONV_CH
    gate = _dot(xb, w_ref[:, c0:c0 + CONV_CH])

    qr = norm_rope(q, gq_ref[...], grp_ref[...]) * (HEAD_DIM ** -0.5 * LOG2_E)
    qT_ref[...] = qr.T.astype(BF16)
    k_ref[...] = norm_rope(k, gk_ref[...], grp_ref[:KV_WIDTH, :KV_WIDTH]).astype(BF16)
    vT = v.T
    ones = jnp.ones((BF16_ROWS, tm), F32)
    for kk in range(ATTN_KV_HEADS):
        vT_ref[kk] = jnp.concatenate(
            [vT[kk * HEAD_DIM:(kk + 1) * HEAD_DIM], ones], axis=0).astype(BF16)
    z_ref[...] = a * jax.nn.sigmoid(gate)


def _l0_in_proj(x, w, gq, gk, cos, sin, grp):
    B, S, D = x.shape
    tm = min(ROW_TILE, S)
    nS = S // tm
    n_in = w.shape[1]
    return pl.pallas_call(
        _l0_in_kernel,
        grid=(B, nS),
        in_specs=[
            pl.BlockSpec((None, tm, D), lambda b, i: (b, i, 0)),
            _resident((D, n_in)),
            _resident((1, ATTN_WIDTH)),
            _resident((1, KV_WIDTH)),
            pl.BlockSpec((tm, LANES), lambda b, i: (i, 0)),
            pl.BlockSpec((tm, LANES), lambda b, i: (i, 0)),
            _resident((ATTN_WIDTH, ATTN_WIDTH)),
        ],
        out_specs=[
            pl.BlockSpec((None, ATTN_WIDTH, tm), lambda b, i: (b, 0, i)),
            pl.BlockSpec((None, tm, KV_WIDTH), lambda b, i: (b, i, 0)),
            pl.BlockSpec((None, ATTN_KV_HEADS, V_ROWS, tm), lambda b, i: (b, 0, 0, i)),
            pl.BlockSpec((None, tm, CONV_CH), lambda b, i: (b, i, 0)),
        ],
        out_shape=[
            jax.ShapeDtypeStruct((B, ATTN_WIDTH, S), BF16),
            jax.ShapeDtypeStruct((B, S, KV_WIDTH), BF16),
            jax.ShapeDtypeStruct((B, ATTN_KV_HEADS, V_ROWS, S), BF16),
            jax.ShapeDtypeStruct((B, S, CONV_CH), F32),
        ],
        compiler_params=_params("parallel", "parallel"),
        name="l0_in_proj",
    )(x, w, gq, gk, cos, sin, grp)


def _attn_kernel(zero_ref, qT_ref, k_ref, vT_ref, o_ref, *scratch, tk):
    U = ATTN_SLOTS
    s_bufs, p_bufs, oT_ref = scratch[:U], scratch[U:2 * U], scratch[2 * U]
    S = k_ref.shape[0]
    tq = qT_ref.shape[1]
    n_k = S // tk
    LA = U - 1
    unroll = ATTN_UNROLL
    assert n_k > LA and unroll % U == 0
    row_head = lax.broadcasted_iota(jnp.int32, (KV_WIDTH, tq), 0) // HEAD_DIM
    staged = pl.ds(pl.multiple_of(zero_ref[0], tk), tk)

    def head(h, carry_unused):
        kk = h // ATTN_GROUP
        row0 = pl.multiple_of(h * HEAD_DIM, HEAD_DIM)
        qh = qT_ref[pl.ds(row0, HEAD_DIM), :]
        q_ext = jnp.where(row_head == kk, jnp.concatenate([qh] * ATTN_KV_HEADS, axis=0), 0)

        def scores(c, slot):
            start = pl.multiple_of(c * tk, tk)
            s = _dot(k_ref[pl.ds(start, tk), :], q_ext).astype(BF16)
            s_bufs[slot][...] = s
            return jnp.max(s, axis=0, keepdims=True).astype(F32)

        def softmax(slot, m_run, m_chunk):
            m_new = jnp.maximum(m_run, m_chunk)
            p_bufs[slot][...] = jnp.exp2(s_bufs[slot][staged, :] - m_new.astype(BF16))
            return m_new, jnp.exp2(m_run - m_new)

        def values(c, slot, acc, alpha):
            start = pl.multiple_of(c * tk, tk)
            v_c = vT_ref[kk, :, pl.ds(start, tk)]
            return acc * alpha + _dot(v_c, p_bufs[slot][...])

        def substep(tau, r, carry, do_values=True, do_scores=True):
            m_run, m_chunks, alpha, acc = carry
            m_chunks = list(m_chunks)
            if do_values:
                acc = values(tau - 1, (r - 1) % U, acc, alpha)
            if do_scores:
                m_chunks[(r + LA) % U] = scores(tau + LA, (r + LA) % U)
            m_run, alpha = softmax(r, m_run, m_chunks[r])
            return m_run, tuple(m_chunks), alpha, acc

        neg_inf = jnp.full((1, tq), -jnp.inf, F32)
        m_chunks = [neg_inf] * U
        for c in range(LA):
            m_chunks[c] = scores(c, c)
        carry = (neg_inf, tuple(m_chunks), neg_inf, jnp.zeros((V_ROWS, tq), F32))
        carry = substep(0, 0, carry, do_values=False)

        def body(i, carry):
            tau0 = 1 + unroll * i
            for u in range(unroll):
                carry = substep(tau0 + u, (1 + u) % U, carry)
            return carry

        n_body = (n_k - LA - 1) // unroll
        carry = lax.fori_loop(0, n_body, body, carry)
        for tau in range(1 + unroll * n_body, n_k):
            carry = substep(tau, tau % U, carry, do_scores=tau + LA < n_k)
        _, _, alpha, acc = carry
        acc = values(n_k - 1, (n_k - 1) % U, acc, alpha)
        oT_ref[pl.ds(row0, HEAD_DIM), :] = acc[:HEAD_DIM] / acc[HEAD_DIM:HEAD_DIM + 1]
        return carry_unused

    lax.fori_loop(0, ATTN_HEADS, head, 0)
    o_ref[...] = oT_ref[...].T.astype(o_ref.dtype)


def _attention(qT, k, vT):
    B, _, S = qT.shape
    tq = min(ATTN_TQ, S)
    tk = min(ATTN_TK, S // 4)
    return pl.pallas_call(
        functools.partial(_attn_kernel, tk=tk),
        grid=(B, S // tq),
        in_specs=[
            pl.BlockSpec(memory_space=pltpu.SMEM),
            pl.BlockSpec((None, ATTN_WIDTH, tq), lambda b, i: (b, 0, i)),
            pl.BlockSpec((None, S, KV_WIDTH), lambda b, i: (b, 0, 0)),
            pl.BlockSpec((None, ATTN_KV_HEADS, V_ROWS, S), lambda b, i: (b, 0, 0, 0)),
        ],
        out_specs=pl.BlockSpec((None, tq, ATTN_WIDTH), lambda b, i: (b, i, 0)),
        out_shape=jax.ShapeDtypeStruct((B, S, ATTN_WIDTH), BF16),
        scratch_shapes=(
            [pltpu.VMEM((tk, tq), BF16) for _ in range(2 * ATTN_SLOTS)]
            + [pltpu.VMEM((ATTN_WIDTH, tq), F32)]
        ),
        compiler_params=_params("parallel", "arbitrary"),
        name="l0_attention",
    )(jnp.zeros((1,), jnp.int32), qT, k, vT)


def _conv_kernel(zp_ref, zc_ref, zn_ref, w_ref, b_ref, g_ref, beta_ref, o_ref, shift_ref):
    ts = zc_ref.shape[0]
    i = pl.program_id(1)
    n = pl.num_programs(1)
    ext = ts + 2 * CONV_HALO - SUBLANES
    prev = jnp.where(i > 0, zp_ref[...], 0.0)
    nxt = jnp.where(i < n - 1, zn_ref[...], 0.0)
    shift_ref[0, 0:CONV_HALO, :] = prev
    shift_ref[0, CONV_HALO:CONV_HALO + ts, :] = zc_ref[...]
    shift_ref[0, CONV_HALO + ts:, :] = nxt
    for r in range(1, SUBLANES):
        shift_ref[r, 0:ext, :] = shift_ref[0, r:r + ext, :]

    bias = b_ref[...]
    gain = g_ref[...]
    beta = beta_ref[...]
    first = CONV_HALO - CONV_PAD

    def block(rb, carry):
        base = pl.multiple_of(rb * CONV_RB, CONV_RB)
        acc = jnp.zeros((CONV_RB, CONV_CH), F32)
        for j in range(CONV_TAPS):
            off = first + j
            r, a = off % SUBLANES, off // SUBLANES
            rows = shift_ref[r, pl.ds(base + a * SUBLANES, CONV_RB), :]
            acc = acc + rows * w_ref[j:j + 1, :]
        y = _layer_norm(acc + bias, gain, beta)
        o_ref[pl.ds(base, CONV_RB), :] = _swish(y).astype(o_ref.dtype)
        return carry

    lax.fori_loop(0, ts // CONV_RB, block, 0)


def _conformer_conv(z, dw_w, dw_b, g, beta):
    B, S, C = z.shape
    ts = min(ROW_TILE, S)
    nS = S // ts
    hb = ts // CONV_HALO
    last_halo = S // CONV_HALO - 1
    row = lambda v: v.reshape(1, C)
    return pl.pallas_call(
        _conv_kernel,
        grid=(B, nS),
        in_specs=[
            pl.BlockSpec((None, CONV_HALO, C), lambda b, i: (b, jnp.maximum(i * hb - 1, 0), 0)),
            pl.BlockSpec((None, ts, C), lambda b, i: (b, i, 0)),
            pl.BlockSpec((None, CONV_HALO, C), lambda b, i: (b, jnp.minimum((i + 1) * hb, last_halo), 0)),
            _resident((CONV_TAPS, C)),
            _resident((1, C)), _resident((1, C)), _resident((1, C)),
        ],
        out_specs=pl.BlockSpec((None, ts, C), lambda b, i: (b, i, 0)),
        out_shape=jax.ShapeDtypeStruct((B, S, C), BF16),
        scratch_shapes=[pltpu.VMEM((SUBLANES, ts + 2 * CONV_HALO, C), F32)],
        compiler_params=_params("parallel", "parallel"),
        name="l0_conv",
    )(z, z, z, dw_w, row(dw_b), row(g), row(beta))


def _out_proj_kernel(*refs, n_parts):
    part_refs = refs[:n_parts]
    x_ref, w_ref, g_ref, b_ref, o_ref = refs[n_parts:]
    out = None
    row0 = 0
    for p_ref in part_refs:
        width = p_ref.shape[1]
        t = _dot(p_ref[...], w_ref[row0:row0 + width, :])
        out = t if out is None else out + t
        row0 += width
    y = DEEPNORM_ALPHA * x_ref[...] + out
    o_ref[...] = _layer_norm(y, g_ref[...], b_ref[...])


def _out_proj_ln(parts, x, w, g, b):
    M, D = x.shape
    tm = min(ROW_TILE, M)
    row = lambda v: v.reshape(1, D)
    return pl.pallas_call(
        functools.partial(_out_proj_kernel, n_parts=len(parts)),
        grid=(M // tm,),
        in_specs=[pl.BlockSpec((tm, p.shape[1]), lambda i: (i, 0)) for p in parts] + [
            pl.BlockSpec((tm, D), lambda i: (i, 0)),
            _resident(w.shape), _resident((1, D)), _resident((1, D)),
        ],
        out_specs=pl.BlockSpec((tm, D), lambda i: (i, 0)),
        out_shape=jax.ShapeDtypeStruct((M, D), F32),
        compiler_params=_params("parallel"),
        name="out_proj_ln",
    )(*parts, x, w, row(g), row(b))


def _ffn_kernel(x_ref, wg_ref, wu_ref, wd_ref, g_ref, b_ref, o_ref, h_ref):
    x = x_ref[...]
    xb = x.astype(BF16)
    hidden = wg_ref.shape[1]
    for c in range(hidden // FFN_HC):
        cols = slice(c * FFN_HC, (c + 1) * FFN_HC)
        gate = _dot(xb, wg_ref[:, cols])
        up = _dot(xb, wu_ref[:, cols])
        h_ref[:, cols] = (_swish(gate) * up).astype(BF16)
    y = DEEPNORM_ALPHA * x + _dot(h_ref[...], wd_ref[...])
    o_ref[...] = _layer_norm(y, g_ref[...], b_ref[...])


def _ffn_ln(x, wg, wu, wd, g, b):
    M, D = x.shape
    hidden = wg.shape[1]
    assert hidden % FFN_HC == 0
    tm = min(ROW_TILE, M)
    row = lambda v: v.reshape(1, D)
    return pl.pallas_call(
        _ffn_kernel,
        grid=(M // tm,),
        in_specs=[
            pl.BlockSpec((tm, D), lambda i: (i, 0)),
            _resident(wg.shape), _resident(wu.shape), _resident(wd.shape),
            _resident((1, D)), _resident((1, D)),
        ],
        out_specs=pl.BlockSpec((tm, D), lambda i: (i, 0)),
        out_shape=jax.ShapeDtypeStruct((M, D), F32),
        scratch_shapes=[pltpu.VMEM((tm, hidden), BF16)],
        compiler_params=_params("parallel"),
        name="ffn_ln",
    )(x, wg, wu, wd, row(g), row(b))


def _l1_in_kernel(x_ref, w_ref, cos_ref, sin_ref, q_ref, kT_ref, v_ref, sg_ref):
    xb = x_ref[...].astype(BF16)
    cos = cos_ref[...]
    sin = sin_ref[...]
    half = RET_DK // 2

    def rope_head(t):
        t1, t2 = t[:, :half], t[:, half:]
        return jnp.concatenate([t1 * cos - t2 * sin, t1 * sin + t2 * cos], axis=1)

    for h in range(RET_HEADS):
        cols = slice(h * RET_DK, (h + 1) * RET_DK)
        q_ref[:, cols] = rope_head(_dot(xb, w_ref[:, cols])).astype(BF16)
    for h in range(RET_HEADS):
        c0 = RET_QK_WIDTH + h * RET_DK
        kh = rope_head(_dot(xb, w_ref[:, c0:c0 + RET_DK])) * (RET_DK ** -0.5)
        kT_ref[h * RET_DK:(h + 1) * RET_DK, :] = kh.T.astype(BF16)
    for h in range(RET_HEADS):
        c0 = 2 * RET_QK_WIDTH + h * RET_DV
        v_ref[:, h * RET_DV:(h + 1) * RET_DV] = _dot(xb, w_ref[:, c0:c0 + RET_DV]).astype(BF16)
    for h in range(RET_HEADS):
        c0 = 2 * RET_QK_WIDTH + RET_V_WIDTH + h * RET_DV
        sg_ref[:, h * RET_DV:(h + 1) * RET_DV] = _swish(_dot(xb, w_ref[:, c0:c0 + RET_DV])).astype(BF16)


def _l1_in_proj(x, w, cos, sin):
    B, S, D = x.shape
    tm = min(ROW_TILE, S)
    return pl.pallas_call(
        _l1_in_kernel,
        grid=(B, S // tm),
        in_specs=[
            pl.BlockSpec((None, tm, D), lambda b, i: (b, i, 0)),
            _resident(w.shape),
            pl.BlockSpec((tm, LANES), lambda b, i: (i, 0)),
            pl.BlockSpec((tm, LANES), lambda b, i: (i, 0)),
        ],
        out_specs=[
            pl.BlockSpec((None, tm, RET_QK_WIDTH), lambda b, i: (b, i, 0)),
            pl.BlockSpec((None, RET_QK_WIDTH, tm), lambda b, i: (b, 0, i)),
            pl.BlockSpec((None, tm, RET_V_WIDTH), lambda b, i: (b, i, 0)),
            pl.BlockSpec((None, tm, RET_V_WIDTH), lambda b, i: (b, i, 0)),
        ],
        out_shape=[
            jax.ShapeDtypeStruct((B, S, RET_QK_WIDTH), BF16),
            jax.ShapeDtypeStruct((B, RET_QK_WIDTH, S), BF16),
            jax.ShapeDtypeStruct((B, S, RET_V_WIDTH), BF16),
            jax.ShapeDtypeStruct((B, S, RET_V_WIDTH), BF16),
        ],
        compiler_params=_params("parallel", "parallel"),
        name="l1_in_proj",
    )(x, w, cos, sin)


def _ret_bwd_state_kernel(lg_ref, kT_ref, v_ref, sb_ref, state_ref):
    h = pl.program_id(1)
    n_c = sb_ref.shape[0]
    C = RET_CHUNK
    lg = lg_ref[h]

    @pl.when(pl.program_id(2) == 0)
    def _():
        state_ref[...] = jnp.zeros_like(state_ref)

    pos = lax.broadcasted_iota(jnp.int32, (1, C), 1).astype(F32)
    k_decay = jnp.exp(lg * pos)
    chunk_decay = jnp.exp(jnp.full((1, RET_DV), lg * C, F32))
    for c in reversed(range(n_c)):
        tok = slice(c * C, (c + 1) * C)
        sb_ref[c] = state_ref[...].astype(BF16)
        k_dec = (kT_ref[:, tok].astype(F32) * k_decay).astype(BF16)
        state_ref[...] = state_ref[...] * chunk_decay + _dot(k_dec, v_ref[tok, :])


def _ret_main_kernel(lgf_ref, lgb_ref, q_ref, kT_ref, v_ref, sg_ref, sb_ref, gn_ref, y_ref, state_ref):
    h = pl.program_id(1)
    n_c = sb_ref.shape[0]
    C = RET_CHUNK
    lgf = lgf_ref[h]
    lgb = lgb_ref[h]

    @pl.when(pl.program_id(2) == 0)
    def _():
        state_ref[...] = jnp.zeros_like(state_ref)

    row = lax.broadcasted_iota(jnp.int32, (C, C), 0).astype(F32)
    col = lax.broadcasted_iota(jnp.int32, (C, C), 1).astype(F32)
    diff = row - col
    intra = jnp.exp(lgf * jnp.maximum(diff, 0.0) + lgb * jnp.maximum(-diff, 0.0))
    q_decay_f = jnp.exp(lgf * (row + 1.0))
    q_decay_b = jnp.exp(lgb * (C - row))
    pos = lax.broadcasted_iota(jnp.int32, (1, C), 1).astype(F32)
    k_decay_f = jnp.exp(lgf * (C - 1.0 - pos))
    chunk_decay_f = jnp.exp(jnp.full((1, RET_DV), lgf * C, F32))
    gn = gn_ref[...]

    for c in range(n_c):
        tok = slice(c * C, (c + 1) * C)
        qc = q_ref[tok, :]
        kTc = kT_ref[:, tok]
        vc = v_ref[tok, :]
        qf32 = qc.astype(F32)
        p = (_dot(qc, kTc) * intra).astype(BF16)
        o = (_dot(p, vc)
             + _dot((qf32 * q_decay_f).astype(BF16), state_ref[...].astype(BF16))
             + _dot((qf32 * q_decay_b).astype(BF16), sb_ref[c]))
        k_dec = (kTc.astype(F32) * k_decay_f).astype(BF16)
        state_ref[...] = state_ref[...] * chunk_decay_f + _dot(k_dec, vc)
        mu = jnp.mean(o, axis=-1, keepdims=True)
        d = o - mu
        var = jnp.mean(d * d, axis=-1, keepdims=True)
        y = d * lax.rsqrt(var + LN_EPS) * gn
        y_ref[tok, :] = (sg_ref[tok, :].astype(F32) * y).astype(y_ref.dtype)


def _retention(q, kT, v, sg, lgf, lgb, gn):
    B, S, _ = q.shape
    ts = min(RET_STEP, S)
    nS = S // ts
    n_c = ts // RET_CHUNK
    smem = pl.BlockSpec(memory_space=pltpu.SMEM)
    sb = pl.pallas_call(
        _ret_bwd_state_kernel,
        grid=(B, RET_HEADS, nS),
        in_specs=[
            smem,
            pl.BlockSpec((None, RET_DK, ts), lambda b, h, i: (b, h, nS - 1 - i)),
            pl.BlockSpec((None, ts, RET_DV), lambda b, h, i: (b, nS - 1 - i, h)),
        ],
        out_specs=pl.BlockSpec((None, None, n_c, RET_DK, RET_DV), lambda b, h, i: (b, h, nS - 1 - i, 0, 0)),
        out_shape=jax.ShapeDtypeStruct((B, RET_HEADS, S // RET_CHUNK, RET_DK, RET_DV), BF16),
        scratch_shapes=[pltpu.VMEM((RET_DK, RET_DV), F32)],
        compiler_params=_params("parallel", "parallel", "arbitrary"),
        name="l1_ret_bwd_state",
    )(lgb, kT, v)
    return pl.pallas_call(
        _ret_main_kernel,
        grid=(B, RET_HEADS, nS),
        in_specs=[
            smem, smem,
            pl.BlockSpec((None, ts, RET_DK), lambda b, h, i: (b, i, h)),
            pl.BlockSpec((None, RET_DK, ts), lambda b, h, i: (b, h, i)),
            pl.BlockSpec((None, ts, RET_DV), lambda b, h, i: (b, i, h)),
            pl.BlockSpec((None, ts, RET_DV), lambda b, h, i: (b, i, h)),
            pl.BlockSpec((None, None, n_c, RET_DK, RET_DV), lambda b, h, i: (b, h, i, 0, 0)),
            pl.BlockSpec((1, RET_DV), lambda b, h, i: (0, h)),
        ],
        out_specs=pl.BlockSpec((None, ts, RET_DV), lambda b, h, i: (b, i, h)),
        out_shape=jax.ShapeDtypeStruct((B, S, RET_V_WIDTH), BF16),
        scratch_shapes=[pltpu.VMEM((RET_DK, RET_DV), F32)],
        compiler_params=_params("parallel", "parallel", "arbitrary"),
        name="l1_retention",
    )(lgf, lgb, q, kT, v, sg, sb, gn.reshape(1, RET_V_WIDTH))


def _axial_rope(seq_len, head_dim):
    rows = seq_len // GRID_W
    row = jnp.broadcast_to(jnp.arange(rows, dtype=F32)[:, None], (rows, GRID_W)).reshape(seq_len)
    col = jnp.broadcast_to(jnp.arange(GRID_W, dtype=F32)[None, :], (rows, GRID_W)).reshape(seq_len)
    axis_dim = head_dim // 2
    inv_freq = ROPE_THETA ** (-jnp.arange(0, axis_dim, 2, dtype=F32) / axis_dim)
    ang = jnp.concatenate([row[:, None] * inv_freq, col[:, None] * inv_freq], axis=-1)
    return jnp.cos(ang), jnp.sin(ang)


def kernel(x, l0_w_in, l0_q_norm_g, l0_k_norm_g, l0_dw_w, l0_dw_b, l0_conv_norm_g, l0_conv_norm_b,
           l0_w_out, l0_ln_mix_g, l0_ln_mix_b, l0_ffn_w_gate, l0_ffn_w_up, l0_ffn_w_down,
           l0_ln_ffn_g, l0_ln_ffn_b, l1_w_in, l1_log_decay_fwd, l1_log_decay_bwd, l1_ret_norm_g,
           l1_w_out, l1_ln_mix_g, l1_ln_mix_b, l1_ffn_w_gate, l1_ffn_w_up, l1_ffn_w_down,
           l1_ln_ffn_g, l1_ln_ffn_b):
    B, S, D = x.shape
    M = B * S
    bf = lambda w: w.astype(BF16)

    cos_a, sin_a = _axial_rope(S, HEAD_DIM)
    cos_l = jnp.tile(jnp.concatenate([cos_a, cos_a], axis=-1), (1, LANES // HEAD_DIM))
    sin_l = jnp.tile(jnp.concatenate([-sin_a, sin_a], axis=-1), (1, LANES // HEAD_DIM))
    cos_r, sin_r = _axial_rope(S, RET_DK)
    head_id = jnp.arange(ATTN_WIDTH) // HEAD_DIM
    grp = (head_id[:, None] == head_id[None, :]).astype(BF16)

    qT, k, vT, z = _l0_in_proj(
        x, bf(l0_w_in),
        jnp.tile(l0_q_norm_g, ATTN_HEADS).reshape(1, ATTN_WIDTH),
        jnp.tile(l0_k_norm_g, ATTN_KV_HEADS).reshape(1, KV_WIDTH),
        cos_l, sin_l, grp)
    attn = _attention(qT, k, vT)
    conv = _conformer_conv(z, l0_dw_w, l0_dw_b, l0_conv_norm_g, l0_conv_norm_b)
    x2d = x.reshape(M, D)
    x2d = _out_proj_ln([attn.reshape(M, ATTN_WIDTH), conv.reshape(M, CONV_CH)], x2d, bf(l0_w_out),
                       l0_ln_mix_g, l0_ln_mix_b)
    x2d = _ffn_ln(x2d, bf(l0_ffn_w_gate), bf(l0_ffn_w_up), bf(l0_ffn_w_down), l0_ln_ffn_g, l0_ln_ffn_b)

    q, kT, v, sg = _l1_in_proj(x2d.reshape(B, S, D), bf(l1_w_in), cos_r, sin_r)
    y = _retention(q, kT, v, sg, l1_log_decay_fwd, l1_log_decay_bwd, l1_ret_norm_g)
    x2d = _out_proj_ln([y.reshape(M, RET_V_WIDTH)], x2d, bf(l1_w_out), l1_ln_mix_g, l1_ln_mix_b)
    x2d = _ffn_ln(x2d, bf(l1_ffn_w_gate), bf(l1_ffn_w_up), bf(l1_ffn_w_down), l1_ln_ffn_g, l1_ln_ffn_b)
    return x2d.reshape(B, S, D)
```

```python
import functools

import jax
import jax.numpy as jnp
from jax import lax
from jax.experimental import pallas as pl
from jax.experimental.pallas import tpu as pltpu

F32 = jnp.float32
BF16 = jnp.bfloat16

GRID_W = 64
ROPE_THETA = 10000.0
ATTN_HEADS = 8
ATTN_KV_HEADS = 2
ATTN_GROUP = ATTN_HEADS // ATTN_KV_HEADS
HEAD_DIM = 64
ATTN_WIDTH = ATTN_HEADS * HEAD_DIM
KV_WIDTH = ATTN_KV_HEADS * HEAD_DIM
CONV_CH = 512
CONV_TAPS = 31
CONV_PAD = CONV_TAPS // 2
RET_HEADS = 4
RET_DK = 256
RET_DV = 512
RET_QK_WIDTH = RET_HEADS * RET_DK
RET_V_WIDTH = RET_HEADS * RET_DV
DEPTH = 2
DEEPNORM_ALPHA = (2 * DEPTH) ** 0.25
LN_EPS = 1e-5
RMS_EPS = 1e-6
LOG2_E = 1.4426950408889634

LANES = 128
SUBLANES = 8
BF16_ROWS = 16
VMEM_LIMIT_BYTES = 56 * 1024 * 1024

ROW_TILE = 512
ATTN_TQ = 256
ATTN_TK = 512
ATTN_SLOTS = 4
ATTN_HEADS_PER_BLOCK = 4
ATTN_UNROLL = 16
V_ROWS = HEAD_DIM + BF16_ROWS
CONV_HALO = 16
CONV_RB = 64
RET_CHUNK = 256
RET_STEP = 2048
FFN_HC = 256


def _dot(a, b):
    return jnp.dot(a, b, preferred_element_type=F32)


def _layer_norm(y, g, b):
    mu = jnp.mean(y, axis=-1, keepdims=True)
    d = y - mu
    var = jnp.mean(d * d, axis=-1, keepdims=True)
    return d * lax.rsqrt(var + LN_EPS) * g + b


def _swish(t):
    return t * jax.nn.sigmoid(t)


def _params(*semantics, flags=None):
    return pltpu.CompilerParams(dimension_semantics=semantics, vmem_limit_bytes=VMEM_LIMIT_BYTES, flags=flags)


def _resident(shape):
    nd = len(shape)
    return pl.BlockSpec(shape, lambda *_: (0,) * nd, pipeline_mode=pl.Buffered(1))


def _l0_in_kernel(x_ref, w_ref, gq_ref, gk_ref, cos_ref, sin_ref, grp_ref,
                  qT_ref, k_ref, vT_ref, z_ref):
    tm = x_ref.shape[0]
    xb = x_ref[...].astype(BF16)
    cos = cos_ref[...]
    sin = sin_ref[...]
    lane = lax.broadcasted_iota(jnp.int32, (tm, LANES), 1)
    first_half = (lane % HEAD_DIM) < (HEAD_DIM // 2)

    def norm_rope(t, g, grp):
        t2 = t * t
        hi = t2.astype(BF16)
        lo = (t2 - hi.astype(F32)).astype(BF16)
        ss = _dot(hi, grp) + _dot(lo, grp)
        tn = t * lax.rsqrt(ss * (1.0 / HEAD_DIM) + RMS_EPS) * g
        outs = []
        for j in range(t.shape[1] // LANES):
            c = tn[:, j * LANES:(j + 1) * LANES]
            partner = jnp.where(first_half,
                                pltpu.roll(c, LANES - HEAD_DIM // 2, 1),
                                pltpu.roll(c, HEAD_DIM // 2, 1))
            outs.append(c * cos + partner * sin)
        return outs[0] if len(outs) == 1 else jnp.concatenate(outs, axis=1)

    c0 = 0
    q = _dot(xb, w_ref[:, c0:c0 + ATTN_WIDTH]); c0 += ATTN_WIDTH
    k = _dot(xb, w_ref[:, c0:c0 + KV_WIDTH]); c0 += KV_WIDTH
    v = _dot(xb, w_ref[:, c0:c0 + KV_WIDTH]); c0 += KV_WIDTH
    a = _dot(xb, w_ref[:, c0:c0 + CONV_CH]); c0 += CONV_CH
    gate = _dot(xb, w_ref[:, c0:c0 + CONV_CH])

    qr = norm_rope(q, gq_ref[...], grp_ref[...]) * (HEAD_DIM ** -0.5 * LOG2_E)
    qT_ref[...] = qr.T.astype(BF16)
    k_ref[...] = norm_rope(k, gk_ref[...], grp_ref[:KV_WIDTH, :KV_WIDTH]).astype(BF16)
    vT = v.T
    ones = jnp.ones((BF16_ROWS, tm), F32)
    for kk in range(ATTN_KV_HEADS):
        vT_ref[kk] = jnp.concatenate(
            [vT[kk * HEAD_DIM:(kk + 1) * HEAD_DIM], ones], axis=0).astype(BF16)
    z_ref[...] = a * jax.nn.sigmoid(gate)


def _l0_in_proj(x, w, gq, gk, cos, sin, grp):
    B, S, D = x.shape
    tm = min(ROW_TILE, S)
    nS = S // tm
    n_in = w.shape[1]
    return pl.pallas_call(
        _l0_in_kernel,
        grid=(B, nS),
        in_specs=[
            pl.BlockSpec((None, tm, D), lambda b, i: (b, i, 0)),
            _resident((D, n_in)),
            _resident((1, ATTN_WIDTH)),
            _resident((1, KV_WIDTH)),
            pl.BlockSpec((tm, LANES), lambda b, i: (i, 0)),
            pl.BlockSpec((tm, LANES), lambda b, i: (i, 0)),
            _resident((ATTN_WIDTH, ATTN_WIDTH)),
        ],
        out_specs=[
            pl.BlockSpec((None, ATTN_WIDTH, tm), lambda b, i: (b, 0, i)),
            pl.BlockSpec((None, tm, KV_WIDTH), lambda b, i: (b, i, 0)),
            pl.BlockSpec((None, ATTN_KV_HEADS, V_ROWS, tm), lambda b, i: (b, 0, 0, i)),
            pl.BlockSpec((None, tm, CONV_CH), lambda b, i: (b, i, 0)),
        ],
        out_shape=[
            jax.ShapeDtypeStruct((B, ATTN_WIDTH, S), BF16),
            jax.ShapeDtypeStruct((B, S, KV_WIDTH), BF16),
            jax.ShapeDtypeStruct((B, ATTN_KV_HEADS, V_ROWS, S), BF16),
            jax.ShapeDtypeStruct((B, S, CONV_CH), F32),
        ],
        compiler_params=_params("parallel", "parallel"),
        name="l0_in_proj",
    )(x, w, gq, gk, cos, sin, grp)


def _attn_kernel(zero_ref, qT_ref, k_ref, vT_ref, o_ref, *scratch, tk):
    U = ATTN_SLOTS
    s_bufs, p_bufs, oT_ref = scratch[:U], scratch[U:2 * U], scratch[2 * U]
    S = k_ref.shape[0]
    tq = qT_ref.shape[1]
    n_k = S // tk
    LA = U - 1
    unroll = ATTN_UNROLL
    assert n_k > LA and unroll % U == 0
    row_head = lax.broadcasted_iota(jnp.int32, (KV_WIDTH, tq), 0) // HEAD_DIM
    staged = pl.ds(pl.multiple_of(zero_ref[0], tk), tk)

    def one_head(h):
        kk = h // ATTN_GROUP
        row0 = pl.multiple_of(h * HEAD_DIM, HEAD_DIM)
        qh = qT_ref[pl.ds(row0, HEAD_DIM), :]
        q_ext = jnp.where(row_head == kk, jnp.concatenate([qh] * ATTN_KV_HEADS, axis=0), 0)

        def scores(c, slot):
            start = pl.multiple_of(c * tk, tk)
            s = _dot(k_ref[pl.ds(start, tk), :], q_ext).astype(BF16)
            s_bufs[slot][...] = s
            return jnp.max(s, axis=0, keepdims=True).astype(F32)

        def softmax(slot, m_run, m_chunk):
            m_new = jnp.maximum(m_run, m_chunk)
            p_bufs[slot][...] = jnp.exp2(s_bufs[slot][staged, :] - m_new.astype(BF16))
            return m_new, jnp.exp2(m_run - m_new)

        def values(c, slot, acc, alpha):
            start = pl.multiple_of(c * tk, tk)
            v_c = vT_ref[kk, :, pl.ds(start, tk)]
            return acc * alpha + _dot(v_c, p_bufs[slot][...])

        def substep(tau, r, carry, do_values=True, do_scores=True):
            m_run, m_chunks, alpha, acc = carry
            m_chunks = list(m_chunks)
            if do_values:
                acc = values(tau - 1, (r - 1) % U, acc, alpha)
            if do_scores:
                m_chunks[(r + LA) % U] = scores(tau + LA, (r + LA) % U)
            m_run, alpha = softmax(r, m_run, m_chunks[r])
            return m_run, tuple(m_chunks), alpha, acc

        neg_inf = jnp.full((1, tq), -jnp.inf, F32)
        m_chunks = [neg_inf] * U
        for c in range(LA):
            m_chunks[c] = scores(c, c)
        carry = (neg_inf, tuple(m_chunks), neg_inf, jnp.zeros((V_ROWS, tq), F32))
        carry = substep(0, 0, carry, do_values=False)

        def body(i, carry):
            tau0 = 1 + unroll * i
            for u in range(unroll):
                carry = substep(tau0 + u, (1 + u) % U, carry)
            return carry

        n_body = (n_k - LA - 1) // unroll
        carry = lax.fori_loop(0, n_body, body, carry)
        for tau in range(1 + unroll * n_body, n_k):
            carry = substep(tau, tau % U, carry, do_scores=tau + LA < n_k)
        _, _, alpha, acc = carry
        acc = values(n_k - 1, (n_k - 1) % U, acc, alpha)
        oT_ref[pl.ds(row0, HEAD_DIM), :] = acc[:HEAD_DIM] / acc[HEAD_DIM:HEAD_DIM + 1]

    def head_block(hb, carry_unused):
        for j in range(ATTN_HEADS_PER_BLOCK):
            one_head(hb * ATTN_HEADS_PER_BLOCK + j)
        return carry_unused

    lax.fori_loop(0, ATTN_HEADS // ATTN_HEADS_PER_BLOCK, head_block, 0)
    o_ref[...] = oT_ref[...].T.astype(o_ref.dtype)


def _attention(qT, k, vT):
    B, _, S = qT.shape
    tq = min(ATTN_TQ, S)
    tk = min(ATTN_TK, S // 4)
    return pl.pallas_call(
        functools.partial(_attn_kernel, tk=tk),
        grid=(B, S // tq),
        in_specs=[
            pl.BlockSpec(memory_space=pltpu.SMEM),
            pl.BlockSpec((None, ATTN_WIDTH, tq), lambda b, i: (b, 0, i)),
            pl.BlockSpec((None, S, KV_WIDTH), lambda b, i: (b, 0, 0)),
            pl.BlockSpec((None, ATTN_KV_HEADS, V_ROWS, S), lambda b, i: (b, 0, 0, 0)),
        ],
        out_specs=pl.BlockSpec((None, tq, ATTN_WIDTH), lambda b, i: (b, i, 0)),
        out_shape=jax.ShapeDtypeStruct((B, S, ATTN_WIDTH), BF16),
        scratch_shapes=(
            [pltpu.VMEM((tk, tq), BF16) for _ in range(2 * ATTN_SLOTS)]
            + [pltpu.VMEM((ATTN_WIDTH, tq), F32)]
        ),
        compiler_params=_params("parallel", "arbitrary"),
        name="l0_attention",
    )(jnp.zeros((1,), jnp.int32), qT, k, vT)


def _conv_kernel(zp_ref, zc_ref, zn_ref, w_ref, b_ref, g_ref, beta_ref, o_ref, shift_ref, y_ref):
    ts = zc_ref.shape[0]
    i = pl.program_id(1)
    n = pl.num_programs(1)
    ext = ts + 2 * CONV_HALO - SUBLANES
    prev = jnp.where(i > 0, zp_ref[...], 0.0)
    nxt = jnp.where(i < n - 1, zn_ref[...], 0.0)
    shift_ref[0, 0:CONV_HALO, :] = prev
    shift_ref[0, CONV_HALO:CONV_HALO + ts, :] = zc_ref[...]
    shift_ref[0, CONV_HALO + ts:, :] = nxt
    for r in range(1, SUBLANES):
        shift_ref[r, 0:ext, :] = shift_ref[0, r:r + ext, :]

    first = CONV_HALO - CONV_PAD
    n_grp = CONV_RB // SUBLANES
    a_max = (first + CONV_TAPS - 1) // SUBLANES

    def block(rb, carry):
        base = pl.multiple_of(rb * CONV_RB, CONV_RB)
        for lane0 in range(0, CONV_CH, LANES):
            lanes = slice(lane0, lane0 + LANES)
            acc = jnp.zeros((n_grp, SUBLANES, LANES), F32)
            for r in range(SUBLANES):
                rows = shift_ref[r, pl.ds(base, CONV_RB + a_max * SUBLANES), lanes]
                rows = rows.reshape(n_grp + a_max, SUBLANES, LANES)
                for a in range(a_max + 1):
                    j = a * SUBLANES + r - first
                    if 0 <= j < CONV_TAPS:
                        acc = acc + rows[a:a + n_grp] * w_ref[j, :, lanes]
            y_ref[pl.ds(base, CONV_RB), lanes] = acc.reshape(CONV_RB, LANES) + b_ref[:, lanes]
        return carry

    lax.fori_loop(0, ts // CONV_RB, block, 0)
    o_ref[...] = _swish(_layer_norm(y_ref[...], g_ref[...], beta_ref[...])).astype(o_ref.dtype)


def _conformer_conv(z, dw_w, dw_b, g, beta):
    B, S, C = z.shape
    ts = min(ROW_TILE, S)
    nS = S // ts
    hb = ts // CONV_HALO
    last_halo = S // CONV_HALO - 1
    row = lambda v: v.reshape(1, C)
    return pl.pallas_call(
        _conv_kernel,
        grid=(B, nS),
        in_specs=[
            pl.BlockSpec((None, CONV_HALO, C), lambda b, i: (b, jnp.maximum(i * hb - 1, 0), 0)),
            pl.BlockSpec((None, ts, C), lambda b, i: (b, i, 0)),
            pl.BlockSpec((None, CONV_HALO, C), lambda b, i: (b, jnp.minimum((i + 1) * hb, last_halo), 0)),
            _resident((CONV_TAPS, SUBLANES, C)),
            _resident((1, C)), _resident((1, C)), _resident((1, C)),
        ],
        out_specs=pl.BlockSpec((None, ts, C), lambda b, i: (b, i, 0)),
        out_shape=jax.ShapeDtypeStruct((B, S, C), BF16),
        scratch_shapes=[pltpu.VMEM((SUBLANES, ts + 2 * CONV_HALO, C), F32), pltpu.VMEM((ts, C), F32)],
        compiler_params=_params("parallel", "parallel"),
        name="l0_conv",
    )(z, z, z, jnp.broadcast_to(dw_w[:, None, :], (CONV_TAPS, SUBLANES, C)), row(dw_b), row(g), row(beta))


def _out_proj_kernel(*refs, n_parts):
    part_refs = refs[:n_parts]
    (x_ref, wo_ref, g1_ref, b1_ref, wg_ref, wu_ref, wd_ref, g2_ref, b2_ref, o_ref, h_ref) = refs[n_parts:]
    out = None
    row0 = 0
    for p_ref in part_refs:
        width = p_ref.shape[1]
        t = _dot(p_ref[...], wo_ref[row0:row0 + width, :])
        out = t if out is None else out + t
        row0 += width
    x = _layer_norm(DEEPNORM_ALPHA * x_ref[...] + out, g1_ref[...], b1_ref[...])
    xb = x.astype(BF16)
    hidden = wg_ref.shape[1]
    for c in range(hidden // FFN_HC):
        cols = slice(c * FFN_HC, (c + 1) * FFN_HC)
        gate = _dot(xb, wg_ref[:, cols])
        up = _dot(xb, wu_ref[:, cols])
        h_ref[:, cols] = (_swish(gate) * up).astype(BF16)
    y = DEEPNORM_ALPHA * x + _dot(h_ref[...], wd_ref[...])
    o_ref[...] = _layer_norm(y, g2_ref[...], b2_ref[...])


def _mix_out_ffn(parts, x, wo, g1, b1, wg, wu, wd, g2, b2):
    M, D = x.shape
    hidden = wg.shape[1]
    assert hidden % FFN_HC == 0
    tm = min(ROW_TILE, M)
    row = lambda v: v.reshape(1, D)
    return pl.pallas_call(
        functools.partial(_out_proj_kernel, n_parts=len(parts)),
        grid=(M // tm,),
        in_specs=[pl.BlockSpec((tm, p.shape[1]), lambda i: (i, 0)) for p in parts] + [
            pl.BlockSpec((tm, D), lambda i: (i, 0)),
            _resident(wo.shape), _resident((1, D)), _resident((1, D)),
            _resident(wg.shape), _resident(wu.shape), _resident(wd.shape),
            _resident((1, D)), _resident((1, D)),
        ],
        out_specs=pl.BlockSpec((tm, D), lambda i: (i, 0)),
        out_shape=jax.ShapeDtypeStruct((M, D), F32),
        scratch_shapes=[pltpu.VMEM((tm, hidden), BF16)],
        compiler_params=_params("parallel"),
        name="mix_out_ffn",
    )(*parts, x, wo, row(g1), row(b1), wg, wu, wd, row(g2), row(b2))


def _l1_in_kernel(x_ref, w_ref, cos_ref, sin_ref, q_ref, kT_ref, v_ref, sg_ref):
    xb = x_ref[...].astype(BF16)
    cos = cos_ref[...]
    sin = sin_ref[...]
    half = RET_DK // 2

    def rope_head(t):
        t1, t2 = t[:, :half], t[:, half:]
        return jnp.concatenate([t1 * cos - t2 * sin, t1 * sin + t2 * cos], axis=1)

    for h in range(RET_HEADS):
        cols = slice(h * RET_DK, (h + 1) * RET_DK)
        q_ref[:, cols] = rope_head(_dot(xb, w_ref[:, cols])).astype(BF16)
    for h in range(RET_HEADS):
        c0 = RET_QK_WIDTH + h * RET_DK
        kh = rope_head(_dot(xb, w_ref[:, c0:c0 + RET_DK])) * (RET_DK ** -0.5)
        kT_ref[h * RET_DK:(h + 1) * RET_DK, :] = kh.T.astype(BF16)
    for h in range(RET_HEADS):
        c0 = 2 * RET_QK_WIDTH + h * RET_DV
        v_ref[:, h * RET_DV:(h + 1) * RET_DV] = _dot(xb, w_ref[:, c0:c0 + RET_DV]).astype(BF16)
    for h in range(RET_HEADS):
        c0 = 2 * RET_QK_WIDTH + RET_V_WIDTH + h * RET_DV
        sg_ref[:, h * RET_DV:(h + 1) * RET_DV] = _swish(_dot(xb, w_ref[:, c0:c0 + RET_DV])).astype(BF16)


def _l1_in_proj(x, w, cos, sin):
    B, S, D = x.shape
    tm = min(ROW_TILE, S)
    return pl.pallas_call(
        _l1_in_kernel,
        grid=(B, S // tm),
        in_specs=[
            pl.BlockSpec((None, tm, D), lambda b, i: (b, i, 0)),
            _resident(w.shape),
            pl.BlockSpec((tm, LANES), lambda b, i: (i, 0)),
            pl.BlockSpec((tm, LANES), lambda b, i: (i, 0)),
        ],
        out_specs=[
            pl.BlockSpec((None, tm, RET_QK_WIDTH), lambda b, i: (b, i, 0)),
            pl.BlockSpec((None, RET_QK_WIDTH, tm), lambda b, i: (b, 0, i)),
            pl.BlockSpec((None, tm, RET_V_WIDTH), lambda b, i: (b, i, 0)),
            pl.BlockSpec((None, tm, RET_V_WIDTH), lambda b, i: (b, i, 0)),
        ],
        out_shape=[
            jax.ShapeDtypeStruct((B, S, RET_QK_WIDTH), BF16),
            jax.ShapeDtypeStruct((B, RET_QK_WIDTH, S), BF16),
            jax.ShapeDtypeStruct((B, S, RET_V_WIDTH), BF16),
            jax.ShapeDtypeStruct((B, S, RET_V_WIDTH), BF16),
        ],
        compiler_params=_params("parallel", "parallel"),
        name="l1_in_proj",
    )(x, w, cos, sin)


def _ret_bwd_state_kernel(lg_ref, kT_ref, v_ref, sb_ref, state_ref):
    h = pl.program_id(1)
    n_c = sb_ref.shape[0]
    C = RET_CHUNK
    lg = lg_ref[h]

    @pl.when(pl.program_id(2) == 0)
    def _():
        state_ref[...] = jnp.zeros_like(state_ref)

    pos = lax.broadcasted_iota(jnp.int32, (1, C), 1).astype(F32)
    k_decay = jnp.exp(lg * pos)
    chunk_decay = jnp.exp(jnp.full((1, RET_DV), lg * C, F32))
    for c in reversed(range(n_c)):
        tok = slice(c * C, (c + 1) * C)
        sb_ref[c] = state_ref[...].astype(BF16)
        k_dec = (kT_ref[:, tok].astype(F32) * k_decay).astype(BF16)
        state_ref[...] = state_ref[...] * chunk_decay + _dot(k_dec, v_ref[tok, :])


def _ret_main_kernel(lgf_ref, lgb_ref, q_ref, kT_ref, v_ref, sg_ref, sb_ref, gn_ref, y_ref, state_ref):
    h = pl.program_id(1)
    n_c = sb_ref.shape[0]
    C = RET_CHUNK
    lgf = lgf_ref[h]
    lgb = lgb_ref[h]

    @pl.when(pl.program_id(2) == 0)
    def _():
        state_ref[...] = jnp.zeros_like(state_ref)

    row = lax.broadcasted_iota(jnp.int32, (C, C), 0).astype(F32)
    col = lax.broadcasted_iota(jnp.int32, (C, C), 1).astype(F32)
    diff = row - col
    intra = jnp.exp(lgf * jnp.maximum(diff, 0.0) + lgb * jnp.maximum(-diff, 0.0))
    q_decay_f = jnp.exp(lgf * (row + 1.0))
    q_decay_b = jnp.exp(lgb * (C - row))
    pos = lax.broadcasted_iota(jnp.int32, (1, C), 1).astype(F32)
    k_decay_f = jnp.exp(lgf * (C - 1.0 - pos))
    chunk_decay_f = jnp.exp(jnp.full((1, RET_DV), lgf * C, F32))
    gn = gn_ref[...]

    for c in range(n_c):
        tok = slice(c * C, (c + 1) * C)
        qc = q_ref[tok, :]
        kTc = kT_ref[:, tok]
        vc = v_ref[tok, :]
        qf32 = qc.astype(F32)
        p = (_dot(qc, kTc) * intra).astype(BF16)
        o = (_dot(p, vc)
             + _dot((qf32 * q_decay_f).astype(BF16), state_ref[...].astype(BF16))
             + _dot((qf32 * q_decay_b).astype(BF16), sb_ref[c]))
        k_dec = (kTc.astype(F32) * k_decay_f).astype(BF16)
        state_ref[...] = state_ref[...] * chunk_decay_f + _dot(k_dec, vc)
        mu = jnp.mean(o, axis=-1, keepdims=True)
        d = o - mu
        var = jnp.mean(d * d, axis=-1, keepdims=True)
        y = d * lax.rsqrt(var + LN_EPS) * gn
        y_ref[tok, :] = (sg_ref[tok, :].astype(F32) * y).astype(y_ref.dtype)


def _retention(q, kT, v, sg, lgf, lgb, gn):
    B, S, _ = q.shape
    ts = min(RET_STEP, S)
    nS = S // ts
    n_c = ts // RET_CHUNK
    smem = pl.BlockSpec(memory_space=pltpu.SMEM)
    sb = pl.pallas_call(
        _ret_bwd_state_kernel,
        grid=(B, RET_HEADS, nS),
        in_specs=[
            smem,
            pl.BlockSpec((None, RET_DK, ts), lambda b, h, i: (b, h, nS - 1 - i)),
            pl.BlockSpec((None, ts, RET_DV), lambda b, h, i: (b, nS - 1 - i, h)),
        ],
        out_specs=pl.BlockSpec((None, None, n_c, RET_DK, RET_DV), lambda b, h, i: (b, h, nS - 1 - i, 0, 0)),
        out_shape=jax.ShapeDtypeStruct((B, RET_HEADS, S // RET_CHUNK, RET_DK, RET_DV), BF16),
        scratch_shapes=[pltpu.VMEM((RET_DK, RET_DV), F32)],
        compiler_params=_params("parallel", "parallel", "arbitrary"),
        name="l1_ret_bwd_state",
    )(lgb, kT, v)
    return pl.pallas_call(
        _ret_main_kernel,
        grid=(B, RET_HEADS, nS),
        in_specs=[
            smem, smem,
            pl.BlockSpec((None, ts, RET_DK), lambda b, h, i: (b, i, h)),
            pl.BlockSpec((None, RET_DK, ts), lambda b, h, i: (b, h, i)),
            pl.BlockSpec((None, ts, RET_DV), lambda b, h, i: (b, i, h)),
            pl.BlockSpec((None, ts, RET_DV), lambda b, h, i: (b, i, h)),
            pl.BlockSpec((None, None, n_c, RET_DK, RET_DV), lambda b, h, i: (b, h, i, 0, 0)),
            pl.BlockSpec((1, RET_DV), lambda b, h, i: (0, h)),
        ],
        out_specs=pl.BlockSpec((None, ts, RET_DV), lambda b, h, i: (b, i, h)),
        out_shape=jax.ShapeDtypeStruct((B, S, RET_V_WIDTH), BF16),
        scratch_shapes=[pltpu.VMEM((RET_DK, RET_DV), F32)],
        compiler_params=_params("parallel", "parallel", "arbitrary"),
        name="l1_retention",
    )(lgf, lgb, q, kT, v, sg, sb, gn.reshape(1, RET_V_WIDTH))


def _axial_rope(seq_len, head_dim):
    rows = seq_len // GRID_W
    row = jnp.broadcast_to(jnp.arange(rows, dtype=F32)[:, None], (rows, GRID_W)).reshape(seq_len)
    col = jnp.broadcast_to(jnp.arange(GRID_W, dtype=F32)[None, :], (rows, GRID_W)).reshape(seq_len)
    axis_dim = head_dim // 2
    inv_freq = ROPE_THETA ** (-jnp.arange(0, axis_dim, 2, dtype=F32) / axis_dim)
    ang = jnp.concatenate([row[:, None] * inv_freq, col[:, None] * inv_freq], axis=-1)
    return jnp.cos(ang), jnp.sin(ang)


def kernel(x, l0_w_in, l0_q_norm_g, l0_k_norm_g, l0_dw_w, l0_dw_b, l0_conv_norm_g, l0_conv_norm_b,
           l0_w_out, l0_ln_mix_g, l0_ln_mix_b, l0_ffn_w_gate, l0_ffn_w_up, l0_ffn_w_down,
           l0_ln_ffn_g, l0_ln_ffn_b, l1_w_in, l1_log_decay_fwd, l1_log_decay_bwd, l1_ret_norm_g,
           l1_w_out, l1_ln_mix_g, l1_ln_mix_b, l1_ffn_w_gate, l1_ffn_w_up, l1_ffn_w_down,
           l1_ln_ffn_g, l1_ln_ffn_b):
    B, S, D = x.shape
    M = B * S
    bf = lambda w: w.astype(BF16)

    cos_a, sin_a = _axial_rope(S, HEAD_DIM)
    cos_l = jnp.tile(jnp.concatenate([cos_a, cos_a], axis=-1), (1, LANES // HEAD_DIM))
    sin_l = jnp.tile(jnp.concatenate([-sin_a, sin_a], axis=-1), (1, LANES // HEAD_DIM))
    cos_r, sin_r = _axial_rope(S, RET_DK)
    head_id = jnp.arange(ATTN_WIDTH) // HEAD_DIM
    grp = (head_id[:, None] == head_id[None, :]).astype(BF16)

    qT, k, vT, z = _l0_in_proj(
        x, bf(l0_w_in),
        jnp.tile(l0_q_norm_g, ATTN_HEADS).reshape(1, ATTN_WIDTH),
        jnp.tile(l0_k_norm_g, ATTN_KV_HEADS).reshape(1, KV_WIDTH),
        cos_l, sin_l, grp)
    attn = _attention(qT, k, vT)
    conv = _conformer_conv(z, l0_dw_w, l0_dw_b, l0_conv_norm_g, l0_conv_norm_b)
    x2d = x.reshape(M, D)
    x2d = _mix_out_ffn([attn.reshape(M, ATTN_WIDTH), conv.reshape(M, CONV_CH)], x2d, bf(l0_w_out),
                       l0_ln_mix_g, l0_ln_mix_b, bf(l0_ffn_w_gate), bf(l0_ffn_w_up), bf(l0_ffn_w_down),
                       l0_ln_ffn_g, l0_ln_ffn_b)

    q, kT, v, sg = _l1_in_proj(x2d.reshape(B, S, D), bf(l1_w_in), cos_r, sin_r)
    y = _retention(q, kT, v, sg, l1_log_decay_fwd, l1_log_decay_bwd, l1_ret_norm_g)
    x2d = _mix_out_ffn([y.reshape(M, RET_V_WIDTH)], x2d, bf(l1_w_out), l1_ln_mix_g, l1_ln_mix_b,
                       bf(l1_ffn_w_gate), bf(l1_ffn_w_up), bf(l1_ffn_w_down), l1_ln_ffn_g, l1_ln_ffn_b)
    return x2d.reshape(B, S, D)
```

```python
import functools

import jax
import jax.numpy as jnp
from jax import lax
from jax.experimental import pallas as pl
from jax.experimental.pallas import tpu as pltpu

F32 = jnp.float32
BF16 = jnp.bfloat16

GRID_W = 64
ROPE_THETA = 10000.0
ATTN_HEADS = 8
ATTN_KV_HEADS = 2
ATTN_GROUP = ATTN_HEADS // ATTN_KV_HEADS
HEAD_DIM = 64
ATTN_WIDTH = ATTN_HEADS * HEAD_DIM
KV_WIDTH = ATTN_KV_HEADS * HEAD_DIM
CONV_CH = 512
CONV_TAPS = 31
CONV_PAD = CONV_TAPS // 2
RET_HEADS = 4
RET_DK = 256
RET_DV = 512
RET_QK_WIDTH = RET_HEADS * RET_DK
RET_V_WIDTH = RET_HEADS * RET_DV
DEPTH = 2
DEEPNORM_ALPHA = (2 * DEPTH) ** 0.25
LN_EPS = 1e-5
RMS_EPS = 1e-6
LOG2_E = 1.4426950408889634

LANES = 128
SUBLANES = 8
BF16_ROWS = 16
VMEM_LIMIT_BYTES = 56 * 1024 * 1024

ROW_TILE = 512
ATTN_TQ = 256
ATTN_TK = 512
ATTN_SLOTS = 4
ATTN_HEADS_PER_BLOCK = 4
ATTN_UNROLL = 16
V_ROWS = HEAD_DIM + BF16_ROWS
CONV_HALO = 16
CONV_RB = 64
RET_CHUNK = 256
RET_STEP = 2048
FFN_HC = 256


def _dot(a, b):
    return jnp.dot(a, b, preferred_element_type=F32)


def _layer_norm(y, g, b):
    mu = jnp.mean(y, axis=-1, keepdims=True)
    d = y - mu
    var = jnp.mean(d * d, axis=-1, keepdims=True)
    return d * lax.rsqrt(var + LN_EPS) * g + b


def _swish(t):
    return t * jax.nn.sigmoid(t)


def _params(*semantics, flags=None):
    return pltpu.CompilerParams(dimension_semantics=semantics, vmem_limit_bytes=VMEM_LIMIT_BYTES, flags=flags)


def _resident(shape):
    nd = len(shape)
    return pl.BlockSpec(shape, lambda *_: (0,) * nd, pipeline_mode=pl.Buffered(1))


def _l0_in_kernel(x_ref, w_ref, gq_ref, gk_ref, cos_ref, sin_ref, grp_ref,
                  qT_ref, k_ref, vT_ref, z_ref):
    tm = x_ref.shape[0]
    xb = x_ref[...].astype(BF16)
    cos = cos_ref[...]
    sin = sin_ref[...]
    lane = lax.broadcasted_iota(jnp.int32, (tm, LANES), 1)
    first_half = (lane % HEAD_DIM) < (HEAD_DIM // 2)

    def norm_rope(t, g, grp):
        t2 = t * t
        hi = t2.astype(BF16)
        lo = (t2 - hi.astype(F32)).astype(BF16)
        ss = _dot(hi, grp) + _dot(lo, grp)
        tn = t * lax.rsqrt(ss * (1.0 / HEAD_DIM) + RMS_EPS) * g
        outs = []
        for j in range(t.shape[1] // LANES):
            c = tn[:, j * LANES:(j + 1) * LANES]
            partner = jnp.where(first_half,
                                pltpu.roll(c, LANES - HEAD_DIM // 2, 1),
                                pltpu.roll(c, HEAD_DIM // 2, 1))
            outs.append(c * cos + partner * sin)
        return outs[0] if len(outs) == 1 else jnp.concatenate(outs, axis=1)

    c0 = 0
    q = _dot(xb, w_ref[:, c0:c0 + ATTN_WIDTH]); c0 += ATTN_WIDTH
    k = _dot(xb, w_ref[:, c0:c0 + KV_WIDTH]); c0 += KV_WIDTH
    v = _dot(xb, w_ref[:, c0:c0 + KV_WIDTH]); c0 += KV_WIDTH
    a = _dot(xb, w_ref[:, c0:c0 + CONV_CH]); c0 += CONV_CH
    gate = _dot(xb, w_ref[:, c0:c0 + CONV_CH])

    qr = norm_rope(q, gq_ref[...], grp_ref[...]) * (HEAD_DIM ** -0.5 * LOG2_E)
    qT_ref[...] = qr.T.astype(BF16)
    k_ref[...] = norm_rope(k, gk_ref[...], grp_ref[:KV_WIDTH, :KV_WIDTH]).astype(BF16)
    vT = v.T
    ones = jnp.ones((BF16_ROWS, tm), F32)
    for kk in range(ATTN_KV_HEADS):
        vT_ref[kk] = jnp.concatenate(
            [vT[kk * HEAD_DIM:(kk + 1) * HEAD_DIM], ones], axis=0).astype(BF16)
    z_ref[...] = a * jax.nn.sigmoid(gate)


def _l0_in_proj(x, w, gq, gk, cos, sin, grp):
    B, S, D = x.shape
    tm = min(ROW_TILE, S)
    nS = S // tm
    n_in = w.shape[1]
    return pl.pallas_call(
        _l0_in_kernel,
        grid=(B, nS),
        in_specs=[
            pl.BlockSpec((None, tm, D), lambda b, i: (b, i, 0)),
            _resident((D, n_in)),
            _resident((1, ATTN_WIDTH)),
            _resident((1, KV_WIDTH)),
            pl.BlockSpec((tm, LANES), lambda b, i: (i, 0)),
            pl.BlockSpec((tm, LANES), lambda b, i: (i, 0)),
            _resident((ATTN_WIDTH, ATTN_WIDTH)),
        ],
        out_specs=[
            pl.BlockSpec((None, ATTN_WIDTH, tm), lambda b, i: (b, 0, i)),
            pl.BlockSpec((None, tm, KV_WIDTH), lambda b, i: (b, i, 0)),
            pl.BlockSpec((None, ATTN_KV_HEADS, V_ROWS, tm), lambda b, i: (b, 0, 0, i)),
            pl.BlockSpec((None, tm, CONV_CH), lambda b, i: (b, i, 0)),
        ],
        out_shape=[
            jax.ShapeDtypeStruct((B, ATTN_WIDTH, S), BF16),
            jax.ShapeDtypeStruct((B, S, KV_WIDTH), BF16),
            jax.ShapeDtypeStruct((B, ATTN_KV_HEADS, V_ROWS, S), BF16),
            jax.ShapeDtypeStruct((B, S, CONV_CH), F32),
        ],
        compiler_params=_params("parallel", "parallel"),
        name="l0_in_proj",
    )(x, w, gq, gk, cos, sin, grp)


def _attn_kernel(zero_ref, qT_ref, k_ref, vT_ref, o_ref, *scratch, tk):
    U = ATTN_SLOTS
    s_bufs, p_bufs, oT_ref = scratch[:U], scratch[U:2 * U], scratch[2 * U]
    S = k_ref.shape[0]
    tq = qT_ref.shape[1]
    n_k = S // tk
    LA = U - 1
    unroll = ATTN_UNROLL
    assert n_k > LA and unroll % U == 0
    row_head = lax.broadcasted_iota(jnp.int32, (KV_WIDTH, tq), 0) // HEAD_DIM
    staged = pl.ds(pl.multiple_of(zero_ref[0], tk), tk)

    def one_head(h):
        kk = h // ATTN_GROUP
        row0 = pl.multiple_of(h * HEAD_DIM, HEAD_DIM)
        qh = qT_ref[pl.ds(row0, HEAD_DIM), :]
        q_ext = jnp.where(row_head == kk, jnp.concatenate([qh] * ATTN_KV_HEADS, axis=0), 0)

        def scores(c, slot):
            start = pl.multiple_of(c * tk, tk)
            s = _dot(k_ref[pl.ds(start, tk), :], q_ext).astype(BF16)
            s_bufs[slot][...] = s
            return jnp.max(s, axis=0, keepdims=True).astype(F32)

        def softmax(slot, m_run, m_chunk):
            m_new = jnp.maximum(m_run, m_chunk)
            p_bufs[slot][...] = jnp.exp2(s_bufs[slot][staged, :] - m_new.astype(BF16))
            return m_new, jnp.exp2(m_run - m_new)

        def values(c, slot, acc, alpha):
            start = pl.multiple_of(c * tk, tk)
            v_c = vT_ref[kk, :, pl.ds(start, tk)]
            return acc * alpha + _dot(v_c, p_bufs[slot][...])

        def substep(tau, r, carry, do_values=True, do_scores=True):
            m_run, m_chunks, alpha, acc = carry
            m_chunks = list(m_chunks)
            if do_values:
                acc = values(tau - 1, (r - 1) % U, acc, alpha)
            if do_scores:
                m_chunks[(r + LA) % U] = scores(tau + LA, (r + LA) % U)
            m_run, alpha = softmax(r, m_run, m_chunks[r])
            return m_run, tuple(m_chunks), alpha, acc

        neg_inf = jnp.full((1, tq), -jnp.inf, F32)
        m_chunks = [neg_inf] * U
        for c in range(LA):
            m_chunks[c] = scores(c, c)
        carry = (neg_inf, tuple(m_chunks), neg_inf, jnp.zeros((V_ROWS, tq), F32))
        carry = substep(0, 0, carry, do_values=False)

        def body(i, carry):
            tau0 = 1 + unroll * i
            for u in range(unroll):
                carry = substep(tau0 + u, (1 + u) % U, carry)
            return carry

        n_body = (n_k - LA - 1) // unroll
        carry = lax.fori_loop(0, n_body, body, carry)
        for tau in range(1 + unroll * n_body, n_k):
            carry = substep(tau, tau % U, carry, do_scores=tau + LA < n_k)
        _, _, alpha, acc = carry
        acc = values(n_k - 1, (n_k - 1) % U, acc, alpha)
        oT_ref[pl.ds(row0, HEAD_DIM), :] = acc[:HEAD_DIM] / acc[HEAD_DIM:HEAD_DIM + 1]

    def head_block(hb, carry_unused):
        for j in range(ATTN_HEADS_PER_BLOCK):
            one_head(hb * ATTN_HEADS_PER_BLOCK + j)
        return carry_unused

    lax.fori_loop(0, ATTN_HEADS // ATTN_HEADS_PER_BLOCK, head_block, 0)
    o_ref[...] = oT_ref[...].T.astype(o_ref.dtype)


def _attention(qT, k, vT):
    B, _, S = qT.shape
    tq = min(ATTN_TQ, S)
    tk = min(ATTN_TK, S // 4)
    return pl.pallas_call(
        functools.partial(_attn_kernel, tk=tk),
        grid=(B, S // tq),
        in_specs=[
            pl.BlockSpec(memory_space=pltpu.SMEM),
            pl.BlockSpec((None, ATTN_WIDTH, tq), lambda b, i: (b, 0, i)),
            pl.BlockSpec((None, S, KV_WIDTH), lambda b, i: (b, 0, 0)),
            pl.BlockSpec((None, ATTN_KV_HEADS, V_ROWS, S), lambda b, i: (b, 0, 0, 0)),
        ],
        out_specs=pl.BlockSpec((None, tq, ATTN_WIDTH), lambda b, i: (b, i, 0)),
        out_shape=jax.ShapeDtypeStruct((B, S, ATTN_WIDTH), BF16),
        scratch_shapes=(
            [pltpu.VMEM((tk, tq), BF16) for _ in range(2 * ATTN_SLOTS)]
            + [pltpu.VMEM((ATTN_WIDTH, tq), F32)]
        ),
        compiler_params=_params("parallel", "arbitrary"),
        name="l0_attention",
    )(jnp.zeros((1,), jnp.int32), qT, k, vT)


def _conv_kernel(zp_ref, zc_ref, zn_ref, w_ref, b_ref, g_ref, beta_ref, o_ref, shift_ref, y_ref):
    ts = zc_ref.shape[0]
    i = pl.program_id(1)
    n = pl.num_programs(1)
    ext = ts + 2 * CONV_HALO - SUBLANES
    prev = jnp.where(i > 0, zp_ref[...], 0.0)
    nxt = jnp.where(i < n - 1, zn_ref[...], 0.0)
    shift_ref[0, 0:CONV_HALO, :] = prev
    shift_ref[0, CONV_HALO:CONV_HALO + ts, :] = zc_ref[...]
    shift_ref[0, CONV_HALO + ts:, :] = nxt
    for r in range(1, SUBLANES):
        shift_ref[r, 0:ext, :] = shift_ref[0, r:r + ext, :]

    first = CONV_HALO - CONV_PAD
    n_grp = CONV_RB // SUBLANES
    a_max = (first + CONV_TAPS - 1) // SUBLANES

    def block(rb, carry):
        base = pl.multiple_of(rb * CONV_RB, CONV_RB)
        for lane0 in range(0, CONV_CH, LANES):
            lanes = slice(lane0, lane0 + LANES)
            acc = jnp.zeros((n_grp, SUBLANES, LANES), F32)
            for r in range(SUBLANES):
                rows = shift_ref[r, pl.ds(base, CONV_RB + a_max * SUBLANES), lanes]
                rows = rows.reshape(n_grp + a_max, SUBLANES, LANES)
                for a in range(a_max + 1):
                    j = a * SUBLANES + r - first
                    if 0 <= j < CONV_TAPS:
                        acc = acc + rows[a:a + n_grp] * w_ref[j, :, lanes]
            y_ref[pl.ds(base, CONV_RB), lanes] = acc.reshape(CONV_RB, LANES) + b_ref[:, lanes]
        return carry

    lax.fori_loop(0, ts // CONV_RB, block, 0)
    o_ref[...] = _swish(_layer_norm(y_ref[...], g_ref[...], beta_ref[...])).astype(o_ref.dtype)


def _conformer_conv(z, dw_w, dw_b, g, beta):
    B, S, C = z.shape
    ts = min(ROW_TILE, S)
    nS = S // ts
    hb = ts // CONV_HALO
    last_halo = S // CONV_HALO - 1
    row = lambda v: v.reshape(1, C)
    return pl.pallas_call(
        _conv_kernel,
        grid=(B, nS),
        in_specs=[
            pl.BlockSpec((None, CONV_HALO, C), lambda b, i: (b, jnp.maximum(i * hb - 1, 0), 0)),
            pl.BlockSpec((None, ts, C), lambda b, i: (b, i, 0)),
            pl.BlockSpec((None, CONV_HALO, C), lambda b, i: (b, jnp.minimum((i + 1) * hb, last_halo), 0)),
            _resident((CONV_TAPS, SUBLANES, C)),
            _resident((1, C)), _resident((1, C)), _resident((1, C)),
        ],
        out_specs=pl.BlockSpec((None, ts, C), lambda b, i: (b, i, 0)),
        out_shape=jax.ShapeDtypeStruct((B, S, C), BF16),
        scratch_shapes=[pltpu.VMEM((SUBLANES, ts + 2 * CONV_HALO, C), F32), pltpu.VMEM((ts, C), F32)],
        compiler_params=_params("parallel", "parallel"),
        name="l0_conv",
    )(z, z, z, jnp.broadcast_to(dw_w[:, None, :], (CONV_TAPS, SUBLANES, C)), row(dw_b), row(g), row(beta))


def _out_proj_kernel(*refs, n_parts):
    part_refs = refs[:n_parts]
    (x_ref, wo_ref, g1_ref, b1_ref, wg_ref, wu_ref, wd_ref, g2_ref, b2_ref, o_ref, h_ref) = refs[n_parts:]
    out = None
    row0 = 0
    for p_ref in part_refs:
        width = p_ref.shape[1]
        t = _dot(p_ref[...], wo_ref[row0:row0 + width, :])
        out = t if out is None else out + t
        row0 += width
    x = _layer_norm(DEEPNORM_ALPHA * x_ref[...] + out, g1_ref[...], b1_ref[...])
    xb = x.astype(BF16)
    hidden = wg_ref.shape[1]
    for c in range(hidden // FFN_HC):
        cols = slice(c * FFN_HC, (c + 1) * FFN_HC)
        gate = _dot(xb, wg_ref[:, cols])
        up = _dot(xb, wu_ref[:, cols])
        h_ref[:, cols] = (_swish(gate) * up).astype(BF16)
    y = DEEPNORM_ALPHA * x + _dot(h_ref[...], wd_ref[...])
    o_ref[...] = _layer_norm(y, g2_ref[...], b2_ref[...])


def _mix_out_ffn(parts, x, wo, g1, b1, wg, wu, wd, g2, b2):
    M, D = x.shape
    hidden = wg.shape[1]
    assert hidden % FFN_HC == 0
    tm = min(ROW_TILE, M)
    row = lambda v: v.reshape(1, D)
    return pl.pallas_call(
        functools.partial(_out_proj_kernel, n_parts=len(parts)),
        grid=(M // tm,),
        in_specs=[pl.BlockSpec((tm, p.shape[1]), lambda i: (i, 0)) for p in parts] + [
            pl.BlockSpec((tm, D), lambda i: (i, 0)),
            _resident(wo.shape), _resident((1, D)), _resident((1, D)),
            _resident(wg.shape), _resident(wu.shape), _resident(wd.shape),
            _resident((1, D)), _resident((1, D)),
        ],
        out_specs=pl.BlockSpec((tm, D), lambda i: (i, 0)),
        out_shape=jax.ShapeDtypeStruct((M, D), F32),
        scratch_shapes=[pltpu.VMEM((tm, hidden), BF16)],
        compiler_params=_params("parallel"),
        name="mix_out_ffn",
    )(*parts, x, wo, row(g1), row(b1), wg, wu, wd, row(g2), row(b2))


def _l1_in_kernel(lgb_ref, x_ref, w_ref, cos_ref, sin_ref, q_ref, kT_ref, v_ref, sg_ref, sb_ref,
                  state_ref, kt32_ref):
    xb = x_ref[...].astype(BF16)
    cos = cos_ref[...]
    sin = sin_ref[...]
    half = RET_DK // 2
    C = RET_CHUNK
    n_c = sb_ref.shape[1]

    @pl.when(pl.program_id(1) == 0)
    def _():
        state_ref[...] = jnp.zeros_like(state_ref)

    def rope_head(t):
        t1, t2 = t[:, :half], t[:, half:]
        return jnp.concatenate([t1 * cos - t2 * sin, t1 * sin + t2 * cos], axis=1)

    for h in range(RET_HEADS):
        c0 = RET_QK_WIDTH + h * RET_DK
        kT = (rope_head(_dot(xb, w_ref[:, c0:c0 + RET_DK])) * (RET_DK ** -0.5)).T
        kT_ref[h * RET_DK:(h + 1) * RET_DK, :] = kT.astype(BF16)
        kt32_ref[h * RET_DK:(h + 1) * RET_DK, :] = kT
    for h in range(RET_HEADS):
        c0 = 2 * RET_QK_WIDTH + h * RET_DV
        v_ref[:, h * RET_DV:(h + 1) * RET_DV] = _dot(xb, w_ref[:, c0:c0 + RET_DV]).astype(BF16)
    for h in range(RET_HEADS):
        cols = slice(h * RET_DK, (h + 1) * RET_DK)
        q_ref[:, cols] = rope_head(_dot(xb, w_ref[:, cols])).astype(BF16)
    for h in range(RET_HEADS):
        c0 = 2 * RET_QK_WIDTH + RET_V_WIDTH + h * RET_DV
        sg_ref[:, h * RET_DV:(h + 1) * RET_DV] = _swish(_dot(xb, w_ref[:, c0:c0 + RET_DV])).astype(BF16)

    pos = lax.broadcasted_iota(jnp.int32, (1, C), 1).astype(F32)
    for h in range(RET_HEADS):
        lg = lgb_ref[h]
        k_decay = jnp.exp(lg * pos)
        chunk_decay = jnp.exp(jnp.full((1, RET_DV), lg * C, F32))
        rows = slice(h * RET_DK, (h + 1) * RET_DK)
        cols = slice(h * RET_DV, (h + 1) * RET_DV)
        updates = [_dot((kt32_ref[rows, c * C:(c + 1) * C] * k_decay).astype(BF16), v_ref[c * C:(c + 1) * C, cols])
                   for c in range(n_c)]
        state = state_ref[h]
        for c in reversed(range(n_c)):
            sb_ref[h, c] = state.astype(BF16)
            state = state * chunk_decay + updates[c]
        state_ref[h] = state


def _l1_in_proj(x, w, cos, sin, lgb):
    B, S, D = x.shape
    tm = min(ROW_TILE, S)
    nS = S // tm
    n_c = tm // RET_CHUNK
    rev = lambda i: nS - 1 - i
    return pl.pallas_call(
        _l1_in_kernel,
        grid=(B, nS),
        in_specs=[
            pl.BlockSpec(memory_space=pltpu.SMEM),
            pl.BlockSpec((None, tm, D), lambda b, i: (b, rev(i), 0)),
            _resident(w.shape),
            pl.BlockSpec((tm, LANES), lambda b, i: (rev(i), 0)),
            pl.BlockSpec((tm, LANES), lambda b, i: (rev(i), 0)),
        ],
        out_specs=[
            pl.BlockSpec((None, tm, RET_QK_WIDTH), lambda b, i: (b, rev(i), 0)),
            pl.BlockSpec((None, RET_QK_WIDTH, tm), lambda b, i: (b, 0, rev(i))),
            pl.BlockSpec((None, tm, RET_V_WIDTH), lambda b, i: (b, rev(i), 0)),
            pl.BlockSpec((None, tm, RET_V_WIDTH), lambda b, i: (b, rev(i), 0)),
            pl.BlockSpec((None, RET_HEADS, n_c, RET_DK, RET_DV), lambda b, i: (b, 0, rev(i), 0, 0)),
        ],
        out_shape=[
            jax.ShapeDtypeStruct((B, S, RET_QK_WIDTH), BF16),
            jax.ShapeDtypeStruct((B, RET_QK_WIDTH, S), BF16),
            jax.ShapeDtypeStruct((B, S, RET_V_WIDTH), BF16),
            jax.ShapeDtypeStruct((B, S, RET_V_WIDTH), BF16),
            jax.ShapeDtypeStruct((B, RET_HEADS, S // RET_CHUNK, RET_DK, RET_DV), BF16),
        ],
        scratch_shapes=[pltpu.VMEM((RET_HEADS, RET_DK, RET_DV), F32), pltpu.VMEM((RET_QK_WIDTH, tm), F32)],
        compiler_params=_params("parallel", "arbitrary"),
        name="l1_in_proj",
    )(lgb, x, w, cos, sin)


def _ret_main_kernel(lgf_ref, lgb_ref, q_ref, kT_ref, v_ref, sg_ref, sb_ref, gn_ref, y_ref, state_ref):
    h = pl.program_id(1)
    n_c = sb_ref.shape[0]
    C = RET_CHUNK
    lgf = lgf_ref[h]
    lgb = lgb_ref[h]

    @pl.when(pl.program_id(2) == 0)
    def _():
        state_ref[...] = jnp.zeros_like(state_ref)

    row = lax.broadcasted_iota(jnp.int32, (C, C), 0).astype(F32)
    col = lax.broadcasted_iota(jnp.int32, (C, C), 1).astype(F32)
    diff = row - col
    intra = jnp.exp(lgf * jnp.maximum(diff, 0.0) + lgb * jnp.maximum(-diff, 0.0))
    q_decay_f = jnp.exp(lgf * (row + 1.0))
    q_decay_b = jnp.exp(lgb * (C - row))
    pos = lax.broadcasted_iota(jnp.int32, (1, C), 1).astype(F32)
    k_decay_f = jnp.exp(lgf * (C - 1.0 - pos))
    chunk_decay_f = jnp.exp(jnp.full((1, RET_DV), lgf * C, F32))
    gn = gn_ref[...]

    state = state_ref[...]
    for c in range(n_c):
        tok = slice(c * C, (c + 1) * C)
        qc = q_ref[tok, :]
        kTc = kT_ref[:, tok]
        vc = v_ref[tok, :]
        qf32 = qc.astype(F32)
        scores = _dot(qc, kTc)
        inter = (_dot((qf32 * q_decay_f).astype(BF16), state.astype(BF16))
                 + _dot((qf32 * q_decay_b).astype(BF16), sb_ref[c]))
        k_dec = (kTc.astype(F32) * k_decay_f).astype(BF16)
        state = state * chunk_decay_f + _dot(k_dec, vc)
        o = inter + _dot((scores * intra).astype(BF16), vc)
        mu = jnp.mean(o, axis=-1, keepdims=True)
        d = o - mu
        var = jnp.mean(d * d, axis=-1, keepdims=True)
        y = d * lax.rsqrt(var + LN_EPS) * gn
        y_ref[tok, :] = (sg_ref[tok, :].astype(F32) * y).astype(y_ref.dtype)
    state_ref[...] = state


def _retention(q, kT, v, sg, sb, lgf, lgb, gn):
    B, S, _ = q.shape
    ts = min(RET_STEP, S)
    nS = S // ts
    n_c = ts // RET_CHUNK
    smem = pl.BlockSpec(memory_space=pltpu.SMEM)
    return pl.pallas_call(
        _ret_main_kernel,
        grid=(B, RET_HEADS, nS),
        in_specs=[
            smem, smem,
            pl.BlockSpec((None, ts, RET_DK), lambda b, h, i: (b, i, h)),
            pl.BlockSpec((None, RET_DK, ts), lambda b, h, i: (b, h, i)),
            pl.BlockSpec((None, ts, RET_DV), lambda b, h, i: (b, i, h)),
            pl.BlockSpec((None, ts, RET_DV), lambda b, h, i: (b, i, h)),
            pl.BlockSpec((None, None, n_c, RET_DK, RET_DV), lambda b, h, i: (b, h, i, 0, 0)),
            pl.BlockSpec((1, RET_DV), lambda b, h, i: (0, h)),
        ],
        out_specs=pl.BlockSpec((None, ts, RET_DV), lambda b, h, i: (b, i, h)),
        out_shape=jax.ShapeDtypeStruct((B, S, RET_V_WIDTH), BF16),
        scratch_shapes=[pltpu.VMEM((RET_DK, RET_DV), F32)],
        compiler_params=_params("parallel", "parallel", "arbitrary"),
        name="l1_retention",
    )(lgf, lgb, q, kT, v, sg, sb, gn.reshape(1, RET_V_WIDTH))


def _axial_rope(seq_len, head_dim):
    rows = seq_len // GRID_W
    axis_dim = head_dim // 2
    inv_freq = ROPE_THETA ** (-jnp.arange(0, axis_dim, 2, dtype=F32) / axis_dim)
    row_ang = jnp.arange(rows, dtype=F32)[:, None] * inv_freq
    col_ang = jnp.arange(GRID_W, dtype=F32)[:, None] * inv_freq
    shape = (rows, GRID_W, inv_freq.shape[0])

    def table(fn):
        by_row = jnp.broadcast_to(fn(row_ang)[:, None, :], shape)
        by_col = jnp.broadcast_to(fn(col_ang)[None, :, :], shape)
        return jnp.concatenate([by_row, by_col], axis=-1).reshape(seq_len, axis_dim)

    return table(jnp.cos), table(jnp.sin)


def kernel(x, l0_w_in, l0_q_norm_g, l0_k_norm_g, l0_dw_w, l0_dw_b, l0_conv_norm_g, l0_conv_norm_b,
           l0_w_out, l0_ln_mix_g, l0_ln_mix_b, l0_ffn_w_gate, l0_ffn_w_up, l0_ffn_w_down,
           l0_ln_ffn_g, l0_ln_ffn_b, l1_w_in, l1_log_decay_fwd, l1_log_decay_bwd, l1_ret_norm_g,
           l1_w_out, l1_ln_mix_g, l1_ln_mix_b, l1_ffn_w_gate, l1_ffn_w_up, l1_ffn_w_down,
           l1_ln_ffn_g, l1_ln_ffn_b):
    B, S, D = x.shape
    M = B * S
    bf = lambda w: w.astype(BF16)

    cos_a, sin_a = _axial_rope(S, HEAD_DIM)
    cos_l = jnp.concatenate([cos_a, cos_a] * (LANES // HEAD_DIM), axis=-1)
    sin_l = jnp.concatenate([-sin_a, sin_a] * (LANES // HEAD_DIM), axis=-1)
    cos_r, sin_r = _axial_rope(S, RET_DK)
    head_id = jnp.arange(ATTN_WIDTH) // HEAD_DIM
    grp = (head_id[:, None] == head_id[None, :]).astype(BF16)

    qT, k, vT, z = _l0_in_proj(
        x, bf(l0_w_in),
        jnp.tile(l0_q_norm_g, ATTN_HEADS).reshape(1, ATTN_WIDTH),
        jnp.tile(l0_k_norm_g, ATTN_KV_HEADS).reshape(1, KV_WIDTH),
        cos_l, sin_l, grp)
    attn = _attention(qT, k, vT)
    conv = _conformer_conv(z, l0_dw_w, l0_dw_b, l0_conv_norm_g, l0_conv_norm_b)
    x2d = x.reshape(M, D)
    x2d = _mix_out_ffn([attn.reshape(M, ATTN_WIDTH), conv.reshape(M, CONV_CH)], x2d, bf(l0_w_out),
                       l0_ln_mix_g, l0_ln_mix_b, bf(l0_ffn_w_gate), bf(l0_ffn_w_up), bf(l0_ffn_w_down),
                       l0_ln_ffn_g, l0_ln_ffn_b)

    q, kT, v, sg, sb = _l1_in_proj(x2d.reshape(B, S, D), bf(l1_w_in), cos_r, sin_r, l1_log_decay_bwd)
    y = _retention(q, kT, v, sg, sb, l1_log_decay_fwd, l1_log_decay_bwd, l1_ret_norm_g)
    x2d = _mix_out_ffn([y.reshape(M, RET_V_WIDTH)], x2d, bf(l1_w_out), l1_ln_mix_g, l1_ln_mix_b,
                       bf(l1_ffn_w_gate), bf(l1_ffn_w_up), bf(l1_ffn_w_down), l1_ln_ffn_g, l1_ln_ffn_b)
    return x2d.reshape(B, S, D)
```

```python
import functools

import jax
import jax.numpy as jnp
from jax import lax
from jax.experimental import pallas as pl
from jax.experimental.pallas import tpu as pltpu

F32 = jnp.float32
BF16 = jnp.bfloat16

GRID_W = 64
ROPE_THETA = 10000.0
ATTN_HEADS = 8
ATTN_KV_HEADS = 2
ATTN_GROUP = ATTN_HEADS // ATTN_KV_HEADS
HEAD_DIM = 64
ATTN_WIDTH = ATTN_HEADS * HEAD_DIM
KV_WIDTH = ATTN_KV_HEADS * HEAD_DIM
CONV_CH = 512
CONV_TAPS = 31
CONV_PAD = CONV_TAPS // 2
RET_HEADS = 4
RET_DK = 256
RET_DV = 512
RET_QK_WIDTH = RET_HEADS * RET_DK
RET_V_WIDTH = RET_HEADS * RET_DV
DEPTH = 2
DEEPNORM_ALPHA = (2 * DEPTH) ** 0.25
LN_EPS = 1e-5
RMS_EPS = 1e-6
LOG2_E = 1.4426950408889634

LANES = 128
SUBLANES = 8
BF16_ROWS = 16
VMEM_LIMIT_BYTES = 56 * 1024 * 1024

ROW_TILE = 512
ATTN_TQ = 256
ATTN_TK = 512
ATTN_SLOTS = 4
ATTN_HEADS_PER_BLOCK = 4
ATTN_UNROLL = 16
V_ROWS = HEAD_DIM + BF16_ROWS
CONV_HALO = 16
CONV_RB = 64
RET_CHUNK = 256
RET_STEP = 2048
FFN_HC = 256
MIX_ROW_GROUPS = 2


def _dot(a, b):
    return jnp.dot(a, b, preferred_element_type=F32)


def _layer_norm(y, g, b):
    mu = jnp.mean(y, axis=-1, keepdims=True)
    d = y - mu
    var = jnp.mean(d * d, axis=-1, keepdims=True)
    return d * lax.rsqrt(var + LN_EPS) * g + b


def _swish(t):
    return t * jax.nn.sigmoid(t)


def _params(*semantics, flags=None):
    return pltpu.CompilerParams(dimension_semantics=semantics, vmem_limit_bytes=VMEM_LIMIT_BYTES, flags=flags)


def _resident(shape):
    nd = len(shape)
    return pl.BlockSpec(shape, lambda *_: (0,) * nd, pipeline_mode=pl.Buffered(1))


def _l0_in_kernel(x_ref, w_ref, gq_ref, gk_ref, cos_ref, sin_ref, grp_ref,
                  qT_ref, k_ref, vT_ref, z_ref):
    tm = x_ref.shape[0]
    xb = x_ref[...].astype(BF16)
    cos = cos_ref[...]
    sin = sin_ref[...]
    lane = lax.broadcasted_iota(jnp.int32, (tm, LANES), 1)
    first_half = (lane % HEAD_DIM) < (HEAD_DIM // 2)

    def head_sum_sq(t, grp):
        t2 = t * t
        hi = t2.astype(BF16)
        lo = (t2 - hi.astype(F32)).astype(BF16)
        return _dot(hi, grp) + _dot(lo, grp)

    def norm_rope(t, ss, g):
        tn = t * lax.rsqrt(ss * (1.0 / HEAD_DIM) + RMS_EPS) * g
        outs = []
        for j in range(t.shape[1] // LANES):
            c = tn[:, j * LANES:(j + 1) * LANES]
            partner = jnp.where(first_half,
                                pltpu.roll(c, LANES - HEAD_DIM // 2, 1),
                                pltpu.roll(c, HEAD_DIM // 2, 1))
            outs.append(c * cos + partner * sin)
        return outs[0] if len(outs) == 1 else jnp.concatenate(outs, axis=1)

    c0 = 0
    q = _dot(xb, w_ref[:, c0:c0 + ATTN_WIDTH]); c0 += ATTN_WIDTH
    k = _dot(xb, w_ref[:, c0:c0 + KV_WIDTH]); c0 += KV_WIDTH
    v = _dot(xb, w_ref[:, c0:c0 + KV_WIDTH]); c0 += KV_WIDTH
    ss_q = head_sum_sq(q, grp_ref[...])
    ss_k = head_sum_sq(k, grp_ref[:KV_WIDTH, :KV_WIDTH])
    a = _dot(xb, w_ref[:, c0:c0 + CONV_CH]); c0 += CONV_CH
    gate = _dot(xb, w_ref[:, c0:c0 + CONV_CH])

    qr = norm_rope(q, ss_q, gq_ref[...]) * (HEAD_DIM ** -0.5 * LOG2_E)
    qT_ref[...] = qr.T.astype(BF16)
    k_ref[...] = norm_rope(k, ss_k, gk_ref[...]).astype(BF16)
    vT = v.T
    ones = jnp.ones((BF16_ROWS, tm), F32)
    for kk in range(ATTN_KV_HEADS):
        vT_ref[kk] = jnp.concatenate(
            [vT[kk * HEAD_DIM:(kk + 1) * HEAD_DIM], ones], axis=0).astype(BF16)
    z_ref[...] = a * jax.nn.sigmoid(gate)


def _l0_in_proj(x, w, gq, gk, cos, sin, grp):
    B, S, D = x.shape
    tm = min(ROW_TILE, S)
    nS = S // tm
    n_in = w.shape[1]
    return pl.pallas_call(
        _l0_in_kernel,
        grid=(B, nS),
        in_specs=[
            pl.BlockSpec((None, tm, D), lambda b, i: (b, i, 0)),
            _resident((D, n_in)),
            _resident((1, ATTN_WIDTH)),
            _resident((1, KV_WIDTH)),
            pl.BlockSpec((tm, LANES), lambda b, i: (i, 0)),
            pl.BlockSpec((tm, LANES), lambda b, i: (i, 0)),
            _resident((ATTN_WIDTH, ATTN_WIDTH)),
        ],
        out_specs=[
            pl.BlockSpec((None, ATTN_WIDTH, tm), lambda b, i: (b, 0, i)),
            pl.BlockSpec((None, tm, KV_WIDTH), lambda b, i: (b, i, 0)),
            pl.BlockSpec((None, ATTN_KV_HEADS, V_ROWS, tm), lambda b, i: (b, 0, 0, i)),
            pl.BlockSpec((None, tm, CONV_CH), lambda b, i: (b, i, 0)),
        ],
        out_shape=[
            jax.ShapeDtypeStruct((B, ATTN_WIDTH, S), BF16),
            jax.ShapeDtypeStruct((B, S, KV_WIDTH), BF16),
            jax.ShapeDtypeStruct((B, ATTN_KV_HEADS, V_ROWS, S), BF16),
            jax.ShapeDtypeStruct((B, S, CONV_CH), F32),
        ],
        compiler_params=_params("parallel", "parallel"),
        name="l0_in_proj",
    )(x, w, gq, gk, cos, sin, grp)


def _attn_kernel(zero_ref, qT_ref, k_ref, vT_ref, o_ref, *scratch, tk):
    U = ATTN_SLOTS
    s_bufs, p_bufs, oT_ref = scratch[:U], scratch[U:2 * U], scratch[2 * U]
    S = k_ref.shape[0]
    tq = qT_ref.shape[1]
    n_k = S // tk
    LA = U - 1
    unroll = ATTN_UNROLL
    assert n_k > LA and unroll % U == 0
    row_head = lax.broadcasted_iota(jnp.int32, (KV_WIDTH, tq), 0) // HEAD_DIM
    staged = pl.ds(pl.multiple_of(zero_ref[0], tk), tk)

    def one_head(h):
        kk = h // ATTN_GROUP
        row0 = pl.multiple_of(h * HEAD_DIM, HEAD_DIM)
        qh = qT_ref[pl.ds(row0, HEAD_DIM), :]
        q_ext = jnp.where(row_head == kk, jnp.concatenate([qh] * ATTN_KV_HEADS, axis=0), 0)

        def scores(c, slot):
            start = pl.multiple_of(c * tk, tk)
            s = _dot(k_ref[pl.ds(start, tk), :], q_ext).astype(BF16)
            s_bufs[slot][...] = s
            return jnp.max(s, axis=0, keepdims=True).astype(F32)

        def softmax(slot, m_run, m_chunk):
            m_new = jnp.maximum(m_run, m_chunk)
            p_bufs[slot][...] = jnp.exp2(s_bufs[slot][staged, :] - m_new.astype(BF16))
            return m_new, jnp.exp2(m_run - m_new)

        def values(c, slot, acc, alpha):
            start = pl.multiple_of(c * tk, tk)
            v_c = vT_ref[kk, :, pl.ds(start, tk)]
            return acc * alpha + _dot(v_c, p_bufs[slot][...])

        def substep(tau, r, carry, do_values=True, do_scores=True):
            m_run, m_chunks, alpha, acc = carry
            m_chunks = list(m_chunks)
            if do_values:
                acc = values(tau - 1, (r - 1) % U, acc, alpha)
            if do_scores:
                m_chunks[(r + LA) % U] = scores(tau + LA, (r + LA) % U)
            m_run, alpha = softmax(r, m_run, m_chunks[r])
            return m_run, tuple(m_chunks), alpha, acc

        neg_inf = jnp.full((1, tq), -jnp.inf, F32)
        m_chunks = [neg_inf] * U
        for c in range(LA):
            m_chunks[c] = scores(c, c)
        carry = (neg_inf, tuple(m_chunks), neg_inf, jnp.zeros((V_ROWS, tq), F32))
        carry = substep(0, 0, carry, do_values=False)

        def body(i, carry):
            tau0 = 1 + unroll * i
            for u in range(unroll):
                carry = substep(tau0 + u, (1 + u) % U, carry)
            return carry

        n_body = (n_k - LA - 1) // unroll
        carry = lax.fori_loop(0, n_body, body, carry)
        for tau in range(1 + unroll * n_body, n_k):
            carry = substep(tau, tau % U, carry, do_scores=tau + LA < n_k)
        _, _, alpha, acc = carry
        acc = values(n_k - 1, (n_k - 1) % U, acc, alpha)
        oT_ref[pl.ds(row0, HEAD_DIM), :] = acc[:HEAD_DIM] / acc[HEAD_DIM:HEAD_DIM + 1]

    def head_block(hb, carry_unused):
        for j in range(ATTN_HEADS_PER_BLOCK):
            one_head(hb * ATTN_HEADS_PER_BLOCK + j)
        return carry_unused

    lax.fori_loop(0, ATTN_HEADS // ATTN_HEADS_PER_BLOCK, head_block, 0)
    o_ref[...] = oT_ref[...].T.astype(o_ref.dtype)


def _attention(qT, k, vT):
    B, _, S = qT.shape
    tq = min(ATTN_TQ, S)
    tk = min(ATTN_TK, S // 4)
    return pl.pallas_call(
        functools.partial(_attn_kernel, tk=tk),
        grid=(B, S // tq),
        in_specs=[
            pl.BlockSpec(memory_space=pltpu.SMEM),
            pl.BlockSpec((None, ATTN_WIDTH, tq), lambda b, i: (b, 0, i)),
            pl.BlockSpec((None, S, KV_WIDTH), lambda b, i: (b, 0, 0)),
            pl.BlockSpec((None, ATTN_KV_HEADS, V_ROWS, S), lambda b, i: (b, 0, 0, 0)),
        ],
        out_specs=pl.BlockSpec((None, tq, ATTN_WIDTH), lambda b, i: (b, i, 0)),
        out_shape=jax.ShapeDtypeStruct((B, S, ATTN_WIDTH), BF16),
        scratch_shapes=(
            [pltpu.VMEM((tk, tq), BF16) for _ in range(2 * ATTN_SLOTS)]
            + [pltpu.VMEM((ATTN_WIDTH, tq), F32)]
        ),
        compiler_params=_params("parallel", "arbitrary"),
        name="l0_attention",
    )(jnp.zeros((1,), jnp.int32), qT, k, vT)


def _conv_kernel(zp_ref, zc_ref, zn_ref, w_ref, b_ref, g_ref, beta_ref, o_ref, shift_ref, y_ref):
    ts = zc_ref.shape[0]
    i = pl.program_id(1)
    n = pl.num_programs(1)
    ext = ts + 2 * CONV_HALO - SUBLANES
    prev = jnp.where(i > 0, zp_ref[...], 0.0)
    nxt = jnp.where(i < n - 1, zn_ref[...], 0.0)
    shift_ref[0, 0:CONV_HALO, :] = prev
    shift_ref[0, CONV_HALO:CONV_HALO + ts, :] = zc_ref[...]
    shift_ref[0, CONV_HALO + ts:, :] = nxt
    for r in range(1, SUBLANES):
        shift_ref[r, 0:ext, :] = shift_ref[0, r:r + ext, :]

    first = CONV_HALO - CONV_PAD
    n_grp = CONV_RB // SUBLANES
    a_max = (first + CONV_TAPS - 1) // SUBLANES

    def block(rb, carry):
        base = pl.multiple_of(rb * CONV_RB, CONV_RB)
        for lane0 in range(0, CONV_CH, LANES):
            lanes = slice(lane0, lane0 + LANES)
            acc = jnp.zeros((n_grp, SUBLANES, LANES), F32)
            for r in range(SUBLANES):
                rows = shift_ref[r, pl.ds(base, CONV_RB + a_max * SUBLANES), lanes]
                rows = rows.reshape(n_grp + a_max, SUBLANES, LANES)
                for a in range(a_max + 1):
                    j = a * SUBLANES + r - first
                    if 0 <= j < CONV_TAPS:
                        acc = acc + rows[a:a + n_grp] * w_ref[j, :, lanes]
            y_ref[pl.ds(base, CONV_RB), lanes] = acc.reshape(CONV_RB, LANES) + b_ref[:, lanes]
        return carry

    lax.fori_loop(0, ts // CONV_RB, block, 0)
    o_ref[...] = _swish(_layer_norm(y_ref[...], g_ref[...], beta_ref[...])).astype(o_ref.dtype)


def _conformer_conv(z, dw_w, dw_b, g, beta):
    B, S, C = z.shape
    ts = min(ROW_TILE, S)
    nS = S // ts
    hb = ts // CONV_HALO
    last_halo = S // CONV_HALO - 1
    row = lambda v: v.reshape(1, C)
    return pl.pallas_call(
        _conv_kernel,
        grid=(B, nS),
        in_specs=[
            pl.BlockSpec((None, CONV_HALO, C), lambda b, i: (b, jnp.maximum(i * hb - 1, 0), 0)),
            pl.BlockSpec((None, ts, C), lambda b, i: (b, i, 0)),
            pl.BlockSpec((None, CONV_HALO, C), lambda b, i: (b, jnp.minimum((i + 1) * hb, last_halo), 0)),
            _resident((CONV_TAPS, SUBLANES, C)),
            _resident((1, C)), _resident((1, C)), _resident((1, C)),
        ],
        out_specs=pl.BlockSpec((None, ts, C), lambda b, i: (b, i, 0)),
        out_shape=jax.ShapeDtypeStruct((B, S, C), BF16),
        scratch_shapes=[pltpu.VMEM((SUBLANES, ts + 2 * CONV_HALO, C), F32), pltpu.VMEM((ts, C), F32)],
        compiler_params=_params("parallel", "parallel"),
        name="l0_conv",
    )(z, z, z, jnp.broadcast_to(dw_w[:, None, :], (CONV_TAPS, SUBLANES, C)), row(dw_b), row(g), row(beta))


def _out_proj_kernel(*refs, n_parts):
    part_refs = refs[:n_parts]
    (x_ref, wo_ref, g1_ref, b1_ref, wg_ref, wu_ref, wd_ref, g2_ref, b2_ref, o_ref, h_ref) = refs[n_parts:]
    tm = x_ref.shape[0]
    n_grp = MIX_ROW_GROUPS if tm % (MIX_ROW_GROUPS * SUBLANES * 2) == 0 else 1
    groups = [slice(r * (tm // n_grp), (r + 1) * (tm // n_grp)) for r in range(n_grp)]
    hidden = wg_ref.shape[1]

    outs = []
    for rows in groups:
        out = None
        row0 = 0
        for p_ref in part_refs:
            width = p_ref.shape[1]
            t = _dot(p_ref[rows, :], wo_ref[row0:row0 + width, :])
            out = t if out is None else out + t
            row0 += width
        outs.append(out)
    xs = [_layer_norm(DEEPNORM_ALPHA * x_ref[rows, :] + out, g1_ref[...], b1_ref[...])
          for rows, out in zip(groups, outs)]
    for rows, x in zip(groups, xs):
        xb = x.astype(BF16)
        for c in range(hidden // FFN_HC):
            cols = slice(c * FFN_HC, (c + 1) * FFN_HC)
            gate = _dot(xb, wg_ref[:, cols])
            up = _dot(xb, wu_ref[:, cols])
            h_ref[rows, cols] = (_swish(gate) * up).astype(BF16)
    for rows, x in zip(groups, xs):
        y = DEEPNORM_ALPHA * x + _dot(h_ref[rows, :], wd_ref[...])
        o_ref[rows, :] = _layer_norm(y, g2_ref[...], b2_ref[...])


def _mix_out_ffn(parts, x, wo, g1, b1, wg, wu, wd, g2, b2):
    M, D = x.shape
    hidden = wg.shape[1]
    assert hidden % FFN_HC == 0
    tm = min(ROW_TILE, M)
    row = lambda v: v.reshape(1, D)
    return pl.pallas_call(
        functools.partial(_out_proj_kernel, n_parts=len(parts)),
        grid=(M // tm,),
        in_specs=[pl.BlockSpec((tm, p.shape[1]), lambda i: (i, 0)) for p in parts] + [
            pl.BlockSpec((tm, D), lambda i: (i, 0)),
            _resident(wo.shape), _resident((1, D)), _resident((1, D)),
            _resident(wg.shape), _resident(wu.shape), _resident(wd.shape),
            _resident((1, D)), _resident((1, D)),
        ],
        out_specs=pl.BlockSpec((tm, D), lambda i: (i, 0)),
        out_shape=jax.ShapeDtypeStruct((M, D), F32),
        scratch_shapes=[pltpu.VMEM((tm, hidden), BF16)],
        compiler_params=_params("parallel"),
        name="mix_out_ffn",
    )(*parts, x, wo, row(g1), row(b1), wg, wu, wd, row(g2), row(b2))


def _l1_in_kernel(lgb_ref, x_ref, w_ref, cos_ref, sin_ref, q_ref, kT_ref, v_ref, sg_ref, sb_ref,
                  state_ref, kt32_ref):
    xb = x_ref[...].astype(BF16)
    cos = cos_ref[...]
    sin = sin_ref[...]
    half = RET_DK // 2
    C = RET_CHUNK
    n_c = sb_ref.shape[1]

    @pl.when(pl.program_id(1) == 0)
    def _():
        state_ref[...] = jnp.zeros_like(state_ref)

    def rope_head(t):
        t1, t2 = t[:, :half], t[:, half:]
        return jnp.concatenate([t1 * cos - t2 * sin, t1 * sin + t2 * cos], axis=1)

    for h in range(RET_HEADS):
        c0 = RET_QK_WIDTH + h * RET_DK
        kT = (rope_head(_dot(xb, w_ref[:, c0:c0 + RET_DK])) * (RET_DK ** -0.5)).T
        kT_ref[h * RET_DK:(h + 1) * RET_DK, :] = kT.astype(BF16)
        kt32_ref[h * RET_DK:(h + 1) * RET_DK, :] = kT
    for h in range(RET_HEADS):
        c0 = 2 * RET_QK_WIDTH + h * RET_DV
        v_ref[:, h * RET_DV:(h + 1) * RET_DV] = _dot(xb, w_ref[:, c0:c0 + RET_DV]).astype(BF16)
    for h in range(RET_HEADS):
        cols = slice(h * RET_DK, (h + 1) * RET_DK)
        q_ref[:, cols] = rope_head(_dot(xb, w_ref[:, cols])).astype(BF16)
    for h in range(RET_HEADS):
        c0 = 2 * RET_QK_WIDTH + RET_V_WIDTH + h * RET_DV
        sg_ref[:, h * RET_DV:(h + 1) * RET_DV] = _swish(_dot(xb, w_ref[:, c0:c0 + RET_DV])).astype(BF16)

    pos = lax.broadcasted_iota(jnp.int32, (1, C), 1).astype(F32)
    for h in range(RET_HEADS):
        lg = lgb_ref[h]
        k_decay = jnp.exp(lg * pos)
        chunk_decay = jnp.exp(jnp.full((1, RET_DV), lg * C, F32))
        rows = slice(h * RET_DK, (h + 1) * RET_DK)
        cols = slice(h * RET_DV, (h + 1) * RET_DV)
        updates = [_dot((kt32_ref[rows, c * C:(c + 1) * C] * k_decay).astype(BF16), v_ref[c * C:(c + 1) * C, cols])
                   for c in range(n_c)]
        state = state_ref[h]
        for c in reversed(range(n_c)):
            sb_ref[h, c] = state.astype(BF16)
            state = state * chunk_decay + updates[c]
        state_ref[h] = state


def _l1_in_proj(x, w, cos, sin, lgb):
    B, S, D = x.shape
    tm = min(ROW_TILE, S)
    nS = S // tm
    n_c = tm // RET_CHUNK
    rev = lambda i: nS - 1 - i
    return pl.pallas_call(
        _l1_in_kernel,
        grid=(B, nS),
        in_specs=[
            pl.BlockSpec(memory_space=pltpu.SMEM),
            pl.BlockSpec((None, tm, D), lambda b, i: (b, rev(i), 0)),
            _resident(w.shape),
            pl.BlockSpec((tm, LANES), lambda b, i: (rev(i), 0)),
            pl.BlockSpec((tm, LANES), lambda b, i: (rev(i), 0)),
        ],
        out_specs=[
            pl.BlockSpec((None, tm, RET_QK_WIDTH), lambda b, i: (b, rev(i), 0)),
            pl.BlockSpec((None, RET_QK_WIDTH, tm), lambda b, i: (b, 0, rev(i))),
            pl.BlockSpec((None, tm, RET_V_WIDTH), lambda b, i: (b, rev(i), 0)),
            pl.BlockSpec((None, tm, RET_V_WIDTH), lambda b, i: (b, rev(i), 0)),
            pl.BlockSpec((None, RET_HEADS, n_c, RET_DK, RET_DV), lambda b, i: (b, 0, rev(i), 0, 0)),
        ],
        out_shape=[
            jax.ShapeDtypeStruct((B, S, RET_QK_WIDTH), BF16),
            jax.ShapeDtypeStruct((B, RET_QK_WIDTH, S), BF16),
            jax.ShapeDtypeStruct((B, S, RET_V_WIDTH), BF16),
            jax.ShapeDtypeStruct((B, S, RET_V_WIDTH), BF16),
            jax.ShapeDtypeStruct((B, RET_HEADS, S // RET_CHUNK, RET_DK, RET_DV), BF16),
        ],
        scratch_shapes=[pltpu.VMEM((RET_HEADS, RET_DK, RET_DV), F32), pltpu.VMEM((RET_QK_WIDTH, tm), F32)],
        compiler_params=_params("parallel", "arbitrary"),
        name="l1_in_proj",
    )(lgb, x, w, cos, sin)


def _ret_main_kernel(lgf_ref, lgb_ref, q_ref, kT_ref, v_ref, sg_ref, sb_ref, gn_ref, y_ref, state_ref):
    h = pl.program_id(1)
    n_c = sb_ref.shape[0]
    C = RET_CHUNK
    lgf = lgf_ref[h]
    lgb = lgb_ref[h]

    @pl.when(pl.program_id(2) == 0)
    def _():
        state_ref[...] = jnp.zeros_like(state_ref)

    row = lax.broadcasted_iota(jnp.int32, (C, C), 0).astype(F32)
    col = lax.broadcasted_iota(jnp.int32, (C, C), 1).astype(F32)
    diff = row - col
    intra = jnp.exp(lgf * jnp.maximum(diff, 0.0) + lgb * jnp.maximum(-diff, 0.0))
    q_decay_f = jnp.exp(lgf * (row + 1.0))
    q_decay_b = jnp.exp(lgb * (C - row))
    pos = lax.broadcasted_iota(jnp.int32, (1, C), 1).astype(F32)
    k_decay_f = jnp.exp(lgf * (C - 1.0 - pos))
    chunk_decay_f = jnp.exp(jnp.full((1, RET_DV), lgf * C, F32))
    gn = gn_ref[...]

    state = state_ref[...]
    for c in range(n_c):
        tok = slice(c * C, (c + 1) * C)
        qc = q_ref[tok, :]
        kTc = kT_ref[:, tok]
        vc = v_ref[tok, :]
        qf32 = qc.astype(F32)
        scores = _dot(qc, kTc)
        inter = (_dot((qf32 * q_decay_f).astype(BF16), state.astype(BF16))
                 + _dot((qf32 * q_decay_b).astype(BF16), sb_ref[c]))
        k_dec = (kTc.astype(F32) * k_decay_f).astype(BF16)
        state = state * chunk_decay_f + _dot(k_dec, vc)
        o = inter + _dot((scores * intra).astype(BF16), vc)
        mu = jnp.mean(o, axis=-1, keepdims=True)
        d = o - mu
        var = jnp.mean(d * d, axis=-1, keepdims=True)
        y = d * lax.rsqrt(var + LN_EPS) * gn
        y_ref[tok, :] = (sg_ref[tok, :].astype(F32) * y).astype(y_ref.dtype)
    state_ref[...] = state


def _retention(q, kT, v, sg, sb, lgf, lgb, gn):
    B, S, _ = q.shape
    ts = min(RET_STEP, S)
    nS = S // ts
    n_c = ts // RET_CHUNK
    smem = pl.BlockSpec(memory_space=pltpu.SMEM)
    return pl.pallas_call(
        _ret_main_kernel,
        grid=(B, RET_HEADS, nS),
        in_specs=[
            smem, smem,
            pl.BlockSpec((None, ts, RET_DK), lambda b, h, i: (b, i, h)),
            pl.BlockSpec((None, RET_DK, ts), lambda b, h, i: (b, h, i)),
            pl.BlockSpec((None, ts, RET_DV), lambda b, h, i: (b, i, h)),
            pl.BlockSpec((None, ts, RET_DV), lambda b, h, i: (b, i, h)),
            pl.BlockSpec((None, None, n_c, RET_DK, RET_DV), lambda b, h, i: (b, h, i, 0, 0)),
            pl.BlockSpec((1, RET_DV), lambda b, h, i: (0, h)),
        ],
        out_specs=pl.BlockSpec((None, ts, RET_DV), lambda b, h, i: (b, i, h)),
        out_shape=jax.ShapeDtypeStruct((B, S, RET_V_WIDTH), BF16),
        scratch_shapes=[pltpu.VMEM((RET_DK, RET_DV), F32)],
        compiler_params=_params("parallel", "parallel", "arbitrary"),
        name="l1_retention",
    )(lgf, lgb, q, kT, v, sg, sb, gn.reshape(1, RET_V_WIDTH))


def _axial_rope_lanes(seq_len, head_dim, sin_sign=None):
    rows = seq_len // GRID_W
    axis_dim = head_dim // 2
    inv_freq = ROPE_THETA ** (-jnp.arange(0, axis_dim, 2, dtype=F32) / axis_dim)
    pair = jnp.arange(LANES) % axis_dim
    freq = jnp.tile(inv_freq, LANES // inv_freq.shape[0])
    is_row = (pair < axis_dim // 2)[None, None, :]
    row_ang = jnp.arange(rows, dtype=F32)[:, None] * freq
    col_ang = jnp.arange(GRID_W, dtype=F32)[:, None] * freq

    def table(fn):
        t = jnp.where(is_row, fn(row_ang)[:, None, :], fn(col_ang)[None, :, :])
        return t.reshape(seq_len, LANES)

    sin = table(jnp.sin)
    return table(jnp.cos), sin if sin_sign is None else sin * sin_sign


def kernel(x, l0_w_in, l0_q_norm_g, l0_k_norm_g, l0_dw_w, l0_dw_b, l0_conv_norm_g, l0_conv_norm_b,
           l0_w_out, l0_ln_mix_g, l0_ln_mix_b, l0_ffn_w_gate, l0_ffn_w_up, l0_ffn_w_down,
           l0_ln_ffn_g, l0_ln_ffn_b, l1_w_in, l1_log_decay_fwd, l1_log_decay_bwd, l1_ret_norm_g,
           l1_w_out, l1_ln_mix_g, l1_ln_mix_b, l1_ffn_w_gate, l1_ffn_w_up, l1_ffn_w_down,
           l1_ln_ffn_g, l1_ln_ffn_b):
    B, S, D = x.shape
    M = B * S
    bf = lambda w: w.astype(BF16)

    rotate_sign = jnp.where(jnp.arange(LANES) % HEAD_DIM < HEAD_DIM // 2, -1.0, 1.0).astype(F32)
    cos_l, sin_l = _axial_rope_lanes(S, HEAD_DIM, rotate_sign)
    cos_r, sin_r = _axial_rope_lanes(S, RET_DK)
    head_id = jnp.arange(ATTN_WIDTH) // HEAD_DIM
    grp = (head_id[:, None] == head_id[None, :]).astype(BF16)

    qT, k, vT, z = _l0_in_proj(
        x, bf(l0_w_in),
        jnp.tile(l0_q_norm_g, ATTN_HEADS).reshape(1, ATTN_WIDTH),
        jnp.tile(l0_k_norm_g, ATTN_KV_HEADS).reshape(1, KV_WIDTH),
        cos_l, sin_l, grp)
    attn = _attention(qT, k, vT)
    conv = _conformer_conv(z, l0_dw_w, l0_dw_b, l0_conv_norm_g, l0_conv_norm_b)
    x2d = x.reshape(M, D)
    x2d = _mix_out_ffn([attn.reshape(M, ATTN_WIDTH), conv.reshape(M, CONV_CH)], x2d, bf(l0_w_out),
                       l0_ln_mix_g, l0_ln_mix_b, bf(l0_ffn_w_gate), bf(l0_ffn_w_up), bf(l0_ffn_w_down),
                       l0_ln_ffn_g, l0_ln_ffn_b)

    q, kT, v, sg, sb = _l1_in_proj(x2d.reshape(B, S, D), bf(l1_w_in), cos_r, sin_r, l1_log_decay_bwd)
    y = _retention(q, kT, v, sg, sb, l1_log_decay_fwd, l1_log_decay_bwd, l1_ret_norm_g)
    x2d = _mix_out_ffn([y.reshape(M, RET_V_WIDTH)], x2d, bf(l1_w_out), l1_ln_mix_g, l1_ln_mix_b,
                       bf(l1_ffn_w_gate), bf(l1_ffn_w_up), bf(l1_ffn_w_down), l1_ln_ffn_g, l1_ln_ffn_b)
    return x2d.reshape(B, S, D)
```

```python
import functools

import jax
import jax.numpy as jnp
from jax import lax
from jax.experimental import pallas as pl
from jax.experimental.pallas import tpu as pltpu

F32 = jnp.float32
BF16 = jnp.bfloat16

GRID_W = 64
ROPE_THETA = 10000.0
ATTN_HEADS = 8
ATTN_KV_HEADS = 2
ATTN_GROUP = ATTN_HEADS // ATTN_KV_HEADS
HEAD_DIM = 64
ATTN_WIDTH = ATTN_HEADS * HEAD_DIM
KV_WIDTH = ATTN_KV_HEADS * HEAD_DIM
CONV_CH = 512
CONV_TAPS = 31
CONV_PAD = CONV_TAPS // 2
RET_HEADS = 4
RET_DK = 256
RET_DV = 512
RET_QK_WIDTH = RET_HEADS * RET_DK
RET_V_WIDTH = RET_HEADS * RET_DV
DEPTH = 2
DEEPNORM_ALPHA = (2 * DEPTH) ** 0.25
LN_EPS = 1e-5
RMS_EPS = 1e-6
LOG2_E = 1.4426950408889634

LANES = 128
SUBLANES = 8
BF16_ROWS = 16
VMEM_LIMIT_BYTES = 56 * 1024 * 1024

ROW_TILE = 512
ATTN_TQ = 256
ATTN_TK = 512
ATTN_SLOTS = 4
ATTN_HEADS_PER_BLOCK = 4
ATTN_UNROLL = 16
V_ROWS = HEAD_DIM + BF16_ROWS
CONV_HALO = 16
CONV_RB = 64
RET_CHUNK = 256
RET_STEP = 2048
FFN_HC = 256
MIX_ROW_GROUPS = 2


def _dot(a, b):
    return jnp.dot(a, b, preferred_element_type=F32)


def _layer_norm(y, g, b):
    mu = jnp.mean(y, axis=-1, keepdims=True)
    d = y - mu
    var = jnp.mean(d * d, axis=-1, keepdims=True)
    return d * lax.rsqrt(var + LN_EPS) * g + b


def _swish(t):
    return t * jax.nn.sigmoid(t)


def _params(*semantics, flags=None):
    return pltpu.CompilerParams(dimension_semantics=semantics, vmem_limit_bytes=VMEM_LIMIT_BYTES, flags=flags)


def _resident(shape):
    nd = len(shape)
    return pl.BlockSpec(shape, lambda *_: (0,) * nd, pipeline_mode=pl.Buffered(1))


def _l0_in_kernel(x_ref, w_ref, gq_ref, gk_ref, rope_row_ref, rope_col_ref, grp_ref,
                  qT_ref, k_ref, vT_ref, z_ref):
    tm = x_ref.shape[0]
    xb = x_ref[...].astype(BF16)
    cos = _rope_tile(rope_row_ref[0], rope_col_ref[0], HEAD_DIM)
    sin = _rope_tile(rope_row_ref[1], rope_col_ref[1], HEAD_DIM)
    lane = lax.broadcasted_iota(jnp.int32, (tm, LANES), 1)
    first_half = (lane % HEAD_DIM) < (HEAD_DIM // 2)

    def head_sum_sq(t, grp):
        t2 = t * t
        hi = t2.astype(BF16)
        lo = (t2 - hi.astype(F32)).astype(BF16)
        return _dot(hi, grp) + _dot(lo, grp)

    def norm_rope(t, ss, g):
        tn = t * lax.rsqrt(ss * (1.0 / HEAD_DIM) + RMS_EPS) * g
        outs = []
        for j in range(t.shape[1] // LANES):
            c = tn[:, j * LANES:(j + 1) * LANES]
            partner = jnp.where(first_half,
                                pltpu.roll(c, LANES - HEAD_DIM // 2, 1),
                                pltpu.roll(c, HEAD_DIM // 2, 1))
            outs.append(c * cos + partner * sin)
        return outs[0] if len(outs) == 1 else jnp.concatenate(outs, axis=1)

    c0 = 0
    q = _dot(xb, w_ref[:, c0:c0 + ATTN_WIDTH]); c0 += ATTN_WIDTH
    k = _dot(xb, w_ref[:, c0:c0 + KV_WIDTH]); c0 += KV_WIDTH
    v = _dot(xb, w_ref[:, c0:c0 + KV_WIDTH]); c0 += KV_WIDTH
    ss_q = head_sum_sq(q, grp_ref[...])
    ss_k = head_sum_sq(k, grp_ref[:KV_WIDTH, :KV_WIDTH])
    a = _dot(xb, w_ref[:, c0:c0 + CONV_CH]); c0 += CONV_CH
    gate = _dot(xb, w_ref[:, c0:c0 + CONV_CH])

    qr = norm_rope(q, ss_q, gq_ref[...]) * (HEAD_DIM ** -0.5 * LOG2_E)
    qT_ref[...] = qr.T.astype(BF16)
    k_ref[...] = norm_rope(k, ss_k, gk_ref[...]).astype(BF16)
    vT = v.T
    ones = jnp.ones((BF16_ROWS, tm), F32)
    for kk in range(ATTN_KV_HEADS):
        vT_ref[kk] = jnp.concatenate(
            [vT[kk * HEAD_DIM:(kk + 1) * HEAD_DIM], ones], axis=0).astype(BF16)
    z_ref[...] = a * jax.nn.sigmoid(gate)


def _l0_in_proj(x, w, gq, gk, rope_row, rope_col, grp):
    B, S, D = x.shape
    tm = min(ROW_TILE, S)
    nS = S // tm
    n_in = w.shape[1]
    assert tm % GRID_W == 0
    return pl.pallas_call(
        _l0_in_kernel,
        grid=(B, nS),
        in_specs=[
            pl.BlockSpec((None, tm, D), lambda b, i: (b, i, 0)),
            _resident((D, n_in)),
            _resident((1, ATTN_WIDTH)),
            _resident((1, KV_WIDTH)),
            pl.BlockSpec((2, tm // GRID_W, LANES), lambda b, i: (0, i, 0)),
            _resident(rope_col.shape),
            _resident((ATTN_WIDTH, ATTN_WIDTH)),
        ],
        out_specs=[
            pl.BlockSpec((None, ATTN_WIDTH, tm), lambda b, i: (b, 0, i)),
            pl.BlockSpec((None, tm, KV_WIDTH), lambda b, i: (b, i, 0)),
            pl.BlockSpec((None, ATTN_KV_HEADS, V_ROWS, tm), lambda b, i: (b, 0, 0, i)),
            pl.BlockSpec((None, tm, CONV_CH), lambda b, i: (b, i, 0)),
        ],
        out_shape=[
            jax.ShapeDtypeStruct((B, ATTN_WIDTH, S), BF16),
            jax.ShapeDtypeStruct((B, S, KV_WIDTH), BF16),
            jax.ShapeDtypeStruct((B, ATTN_KV_HEADS, V_ROWS, S), BF16),
            jax.ShapeDtypeStruct((B, S, CONV_CH), F32),
        ],
        compiler_params=_params("parallel", "parallel"),
        name="l0_in_proj",
    )(x, w, gq, gk, rope_row, rope_col, grp)


def _attn_kernel(zero_ref, qT_ref, k_ref, vT_ref, o_ref, *scratch, tk):
    U = ATTN_SLOTS
    s_bufs, p_bufs, oT_ref = scratch[:U], scratch[U:2 * U], scratch[2 * U]
    S = k_ref.shape[0]
    tq = qT_ref.shape[1]
    n_k = S // tk
    LA = U - 1
    unroll = ATTN_UNROLL
    assert n_k > LA and unroll % U == 0
    row_head = lax.broadcasted_iota(jnp.int32, (KV_WIDTH, tq), 0) // HEAD_DIM
    staged = pl.ds(pl.multiple_of(zero_ref[0], tk), tk)

    def one_head(h):
        kk = h // ATTN_GROUP
        row0 = pl.multiple_of(h * HEAD_DIM, HEAD_DIM)
        qh = qT_ref[pl.ds(row0, HEAD_DIM), :]
        q_ext = jnp.where(row_head == kk, jnp.concatenate([qh] * ATTN_KV_HEADS, axis=0), 0)

        def scores(c, slot):
            start = pl.multiple_of(c * tk, tk)
            s = _dot(k_ref[pl.ds(start, tk), :], q_ext).astype(BF16)
            s_bufs[slot][...] = s
            return jnp.max(s, axis=0, keepdims=True).astype(F32)

        def softmax(slot, m_run, m_chunk):
            m_new = jnp.maximum(m_run, m_chunk)
            p_bufs[slot][...] = jnp.exp2(s_bufs[slot][staged, :] - m_new.astype(BF16))
            return m_new, jnp.exp2(m_run - m_new)

        def values(c, slot, acc, alpha):
            start = pl.multiple_of(c * tk, tk)
            v_c = vT_ref[kk, :, pl.ds(start, tk)]
            return acc * alpha + _dot(v_c, p_bufs[slot][...])

        def substep(tau, r, carry, do_values=True, do_scores=True):
            m_run, m_chunks, alpha, acc = carry
            m_chunks = list(m_chunks)
            if do_values:
                acc = values(tau - 1, (r - 1) % U, acc, alpha)
            if do_scores:
                m_chunks[(r + LA) % U] = scores(tau + LA, (r + LA) % U)
            m_run, alpha = softmax(r, m_run, m_chunks[r])
            return m_run, tuple(m_chunks), alpha, acc

        neg_inf = jnp.full((1, tq), -jnp.inf, F32)
        m_chunks = [neg_inf] * U
        for c in range(LA):
            m_chunks[c] = scores(c, c)
        carry = (neg_inf, tuple(m_chunks), neg_inf, jnp.zeros((V_ROWS, tq), F32))
        carry = substep(0, 0, carry, do_values=False)

        def body(i, carry):
            tau0 = 1 + unroll * i
            for u in range(unroll):
                carry = substep(tau0 + u, (1 + u) % U, carry)
            return carry

        n_body = (n_k - LA - 1) // unroll
        carry = lax.fori_loop(0, n_body, body, carry)
        for tau in range(1 + unroll * n_body, n_k):
            carry = substep(tau, tau % U, carry, do_scores=tau + LA < n_k)
        _, _, alpha, acc = carry
        acc = values(n_k - 1, (n_k - 1) % U, acc, alpha)
        oT_ref[pl.ds(row0, HEAD_DIM), :] = acc[:HEAD_DIM] / acc[HEAD_DIM:HEAD_DIM + 1]

    def head_block(hb, carry_unused):
        for j in range(ATTN_HEADS_PER_BLOCK):
            one_head(hb * ATTN_HEADS_PER_BLOCK + j)
        return carry_unused

    lax.fori_loop(0, ATTN_HEADS // ATTN_HEADS_PER_BLOCK, head_block, 0)
    o_ref[...] = oT_ref[...].T.astype(o_ref.dtype)


def _attention(qT, k, vT):
    B, _, S = qT.shape
    tq = min(ATTN_TQ, S)
    tk = min(ATTN_TK, S // 4)
    return pl.pallas_call(
        functools.partial(_attn_kernel, tk=tk),
        grid=(B, S // tq),
        in_specs=[
            pl.BlockSpec(memory_space=pltpu.SMEM),
            pl.BlockSpec((None, ATTN_WIDTH, tq), lambda b, i: (b, 0, i)),
            pl.BlockSpec((None, S, KV_WIDTH), lambda b, i: (b, 0, 0)),
            pl.BlockSpec((None, ATTN_KV_HEADS, V_ROWS, S), lambda b, i: (b, 0, 0, 0)),
        ],
        out_specs=pl.BlockSpec((None, tq, ATTN_WIDTH), lambda b, i: (b, i, 0)),
        out_shape=jax.ShapeDtypeStruct((B, S, ATTN_WIDTH), BF16),
        scratch_shapes=(
            [pltpu.VMEM((tk, tq), BF16) for _ in range(2 * ATTN_SLOTS)]
            + [pltpu.VMEM((ATTN_WIDTH, tq), F32)]
        ),
        compiler_params=_params("parallel", "arbitrary"),
        name="l0_attention",
    )(jnp.zeros((1,), jnp.int32), qT, k, vT)


def _conv_kernel(zp_ref, zc_ref, zn_ref, w_ref, b_ref, o_ref, shift_ref):
    ts = zc_ref.shape[0]
    i = pl.program_id(1)
    n = pl.num_programs(1)
    ext = ts + 2 * CONV_HALO - SUBLANES
    prev = jnp.where(i > 0, zp_ref[...], 0.0)
    nxt = jnp.where(i < n - 1, zn_ref[...], 0.0)
    shift_ref[0, 0:CONV_HALO, :] = prev
    shift_ref[0, CONV_HALO:CONV_HALO + ts, :] = zc_ref[...]
    shift_ref[0, CONV_HALO + ts:, :] = nxt
    for r in range(1, SUBLANES):
        shift_ref[r, 0:ext, :] = shift_ref[0, r:r + ext, :]

    first = CONV_HALO - CONV_PAD
    n_grp = CONV_RB // SUBLANES
    a_max = (first + CONV_TAPS - 1) // SUBLANES

    def block(rb, carry):
        base = pl.multiple_of(rb * CONV_RB, CONV_RB)
        for lane0 in range(0, CONV_CH, LANES):
            lanes = slice(lane0, lane0 + LANES)
            acc = jnp.zeros((n_grp, SUBLANES, LANES), F32)
            for r in range(SUBLANES):
                rows = shift_ref[r, pl.ds(base, CONV_RB + a_max * SUBLANES), lanes]
                rows = rows.reshape(n_grp + a_max, SUBLANES, LANES)
                for a in range(a_max + 1):
                    j = a * SUBLANES + r - first
                    if 0 <= j < CONV_TAPS:
                        acc = acc + rows[a:a + n_grp] * w_ref[j, :, lanes]
            o_ref[pl.ds(base, CONV_RB), lanes] = acc.reshape(CONV_RB, LANES) + b_ref[:, lanes]
        return carry

    lax.fori_loop(0, ts // CONV_RB, block, 0)


def _conformer_conv(z, dw_w, dw_b):
    B, S, C = z.shape
    ts = min(ROW_TILE, S)
    nS = S // ts
    hb = ts // CONV_HALO
    last_halo = S // CONV_HALO - 1
    row = lambda v: v.reshape(1, C)
    return pl.pallas_call(
        _conv_kernel,
        grid=(B, nS),
        in_specs=[
            pl.BlockSpec((None, CONV_HALO, C), lambda b, i: (b, jnp.maximum(i * hb - 1, 0), 0)),
            pl.BlockSpec((None, ts, C), lambda b, i: (b, i, 0)),
            pl.BlockSpec((None, CONV_HALO, C), lambda b, i: (b, jnp.minimum((i + 1) * hb, last_halo), 0)),
            _resident((CONV_TAPS, SUBLANES, C)),
            _resident((1, C)),
        ],
        out_specs=pl.BlockSpec((None, ts, C), lambda b, i: (b, i, 0)),
        out_shape=jax.ShapeDtypeStruct((B, S, C), F32),
        scratch_shapes=[pltpu.VMEM((SUBLANES, ts + 2 * CONV_HALO, C), F32)],
        compiler_params=_params("parallel", "parallel"),
        name="l0_conv",
    )(z, z, z, jnp.broadcast_to(dw_w[:, None, :], (CONV_TAPS, SUBLANES, C)), row(dw_b))


def _out_proj_kernel(*refs, n_parts, norm_last):
    part_refs, rest = refs[:n_parts], refs[n_parts:]
    if norm_last:
        (ng_ref, nb_ref), rest = rest[:2], rest[2:]
    (x_ref, wo_ref, g1_ref, b1_ref, wg_ref, wu_ref, wd_ref, g2_ref, b2_ref, o_ref, h_ref) = rest
    tm = x_ref.shape[0]
    n_grp = MIX_ROW_GROUPS if tm % (MIX_ROW_GROUPS * SUBLANES * 2) == 0 else 1
    groups = [slice(r * (tm // n_grp), (r + 1) * (tm // n_grp)) for r in range(n_grp)]
    hidden = wg_ref.shape[1]

    outs = []
    for rows in groups:
        out = None
        row0 = 0
        for j, p_ref in enumerate(part_refs):
            width = p_ref.shape[1]
            part = p_ref[rows, :]
            if norm_last and j == n_parts - 1:
                part = _swish(_layer_norm(part, ng_ref[...], nb_ref[...])).astype(BF16)
            t = _dot(part, wo_ref[row0:row0 + width, :])
            out = t if out is None else out + t
            row0 += width
        outs.append(out)
    xs = [_layer_norm(DEEPNORM_ALPHA * x_ref[rows, :] + out, g1_ref[...], b1_ref[...])
          for rows, out in zip(groups, outs)]
    for rows, x in zip(groups, xs):
        xb = x.astype(BF16)
        for c in range(hidden // FFN_HC):
            cols = slice(c * FFN_HC, (c + 1) * FFN_HC)
            gate = _dot(xb, wg_ref[:, cols])
            up = _dot(xb, wu_ref[:, cols])
            h_ref[rows, cols] = (_swish(gate) * up).astype(BF16)
    for rows, x in zip(groups, xs):
        y = DEEPNORM_ALPHA * x + _dot(h_ref[rows, :], wd_ref[...])
        o_ref[rows, :] = _layer_norm(y, g2_ref[...], b2_ref[...])


def _mix_out_ffn(parts, x, wo, g1, b1, wg, wu, wd, g2, b2, last_part_norm=None):
    M, D = x.shape
    hidden = wg.shape[1]
    assert hidden % FFN_HC == 0
    tm = min(ROW_TILE, M)
    row = lambda v: v.reshape(1, -1)
    norm_args = [] if last_part_norm is None else [row(v) for v in last_part_norm]
    return pl.pallas_call(
        functools.partial(_out_proj_kernel, n_parts=len(parts), norm_last=last_part_norm is not None),
        grid=(M // tm,),
        in_specs=[pl.BlockSpec((tm, p.shape[1]), lambda i: (i, 0)) for p in parts]
        + [_resident(v.shape) for v in norm_args] + [
            pl.BlockSpec((tm, D), lambda i: (i, 0)),
            _resident(wo.shape), _resident((1, D)), _resident((1, D)),
            _resident(wg.shape), _resident(wu.shape), _resident(wd.shape),
            _resident((1, D)), _resident((1, D)),
        ],
        out_specs=pl.BlockSpec((tm, D), lambda i: (i, 0)),
        out_shape=jax.ShapeDtypeStruct((M, D), F32),
        scratch_shapes=[pltpu.VMEM((tm, hidden), BF16)],
        compiler_params=_params("parallel"),
        name="mix_out_ffn",
    )(*parts, *norm_args, x, wo, row(g1), row(b1), wg, wu, wd, row(g2), row(b2))


def _l1_in_kernel(lgb_ref, x_ref, w_ref, rope_row_ref, rope_col_ref, q_ref, kT_ref, v_ref, sg_ref, sb_ref,
                  state_ref, kt32_ref):
    xb = x_ref[...].astype(BF16)
    cos = _rope_tile(rope_row_ref[0], rope_col_ref[0], RET_DK)
    sin = _rope_tile(rope_row_ref[1], rope_col_ref[1], RET_DK)
    half = RET_DK // 2
    C = RET_CHUNK
    n_c = sb_ref.shape[1]

    @pl.when(pl.program_id(1) == 0)
    def _():
        state_ref[...] = jnp.zeros_like(state_ref)

    def rope_head(t):
        t1, t2 = t[:, :half], t[:, half:]
        return jnp.concatenate([t1 * cos - t2 * sin, t1 * sin + t2 * cos], axis=1)

    for h in range(RET_HEADS):
        c0 = RET_QK_WIDTH + h * RET_DK
        kT = (rope_head(_dot(xb, w_ref[:, c0:c0 + RET_DK])) * (RET_DK ** -0.5)).T
        kT_ref[h * RET_DK:(h + 1) * RET_DK, :] = kT.astype(BF16)
        kt32_ref[h * RET_DK:(h + 1) * RET_DK, :] = kT
    for h in range(RET_HEADS):
        c0 = 2 * RET_QK_WIDTH + h * RET_DV
        v_ref[:, h * RET_DV:(h + 1) * RET_DV] = _dot(xb, w_ref[:, c0:c0 + RET_DV]).astype(BF16)
    for h in range(RET_HEADS):
        cols = slice(h * RET_DK, (h + 1) * RET_DK)
        q_ref[:, cols] = rope_head(_dot(xb, w_ref[:, cols])).astype(BF16)
    for h in range(RET_HEADS):
        c0 = 2 * RET_QK_WIDTH + RET_V_WIDTH + h * RET_DV
        sg_ref[:, h * RET_DV:(h + 1) * RET_DV] = _swish(_dot(xb, w_ref[:, c0:c0 + RET_DV])).astype(BF16)

    pos = lax.broadcasted_iota(jnp.int32, (1, C), 1).astype(F32)
    for h in range(RET_HEADS):
        lg = lgb_ref[h]
        k_decay = jnp.exp(lg * pos)
        chunk_decay = jnp.exp(jnp.full((1, RET_DV), lg * C, F32))
        rows = slice(h * RET_DK, (h + 1) * RET_DK)
        cols = slice(h * RET_DV, (h + 1) * RET_DV)
        updates = [_dot((kt32_ref[rows, c * C:(c + 1) * C] * k_decay).astype(BF16), v_ref[c * C:(c + 1) * C, cols])
                   for c in range(n_c)]
        state = state_ref[h]
        for c in reversed(range(n_c)):
            sb_ref[h, c] = state.astype(BF16)
            state = state * chunk_decay + updates[c]
        state_ref[h] = state


def _l1_in_proj(x, w, rope_row, rope_col, lgb):
    B, S, D = x.shape
    tm = min(ROW_TILE, S)
    nS = S // tm
    assert tm % GRID_W == 0
    n_c = tm // RET_CHUNK
    rev = lambda i: nS - 1 - i
    return pl.pallas_call(
        _l1_in_kernel,
        grid=(B, nS),
        in_specs=[
            pl.BlockSpec(memory_space=pltpu.SMEM),
            pl.BlockSpec((None, tm, D), lambda b, i: (b, rev(i), 0)),
            _resident(w.shape),
            pl.BlockSpec((2, tm // GRID_W, LANES), lambda b, i: (0, rev(i), 0)),
            _resident(rope_col.shape),
        ],
        out_specs=[
            pl.BlockSpec((None, tm, RET_QK_WIDTH), lambda b, i: (b, rev(i), 0)),
            pl.BlockSpec((None, RET_QK_WIDTH, tm), lambda b, i: (b, 0, rev(i))),
            pl.BlockSpec((None, tm, RET_V_WIDTH), lambda b, i: (b, rev(i), 0)),
            pl.BlockSpec((None, tm, RET_V_WIDTH), lambda b, i: (b, rev(i), 0)),
            pl.BlockSpec((None, RET_HEADS, n_c, RET_DK, RET_DV), lambda b, i: (b, 0, rev(i), 0, 0)),
        ],
        out_shape=[
            jax.ShapeDtypeStruct((B, S, RET_QK_WIDTH), BF16),
            jax.ShapeDtypeStruct((B, RET_QK_WIDTH, S), BF16),
            jax.ShapeDtypeStruct((B, S, RET_V_WIDTH), BF16),
            jax.ShapeDtypeStruct((B, S, RET_V_WIDTH), BF16),
            jax.ShapeDtypeStruct((B, RET_HEADS, S // RET_CHUNK, RET_DK, RET_DV), BF16),
        ],
        scratch_shapes=[pltpu.VMEM((RET_HEADS, RET_DK, RET_DV), F32), pltpu.VMEM((RET_QK_WIDTH, tm), F32)],
        compiler_params=_params("parallel", "arbitrary"),
        name="l1_in_proj",
    )(lgb, x, w, rope_row, rope_col)


def _ret_main_kernel(lgf_ref, lgb_ref, q_ref, kT_ref, v_ref, sg_ref, sb_ref, gn_ref, y_ref, state_ref):
    h = pl.program_id(1)
    n_c = sb_ref.shape[0]
    C = RET_CHUNK
    lgf = lgf_ref[h]
    lgb = lgb_ref[h]

    @pl.when(pl.program_id(2) == 0)
    def _():
        state_ref[...] = jnp.zeros_like(state_ref)

    row = lax.broadcasted_iota(jnp.int32, (C, C), 0).astype(F32)
    col = lax.broadcasted_iota(jnp.int32, (C, C), 1).astype(F32)
    diff = row - col
    intra = jnp.exp(lgf * jnp.maximum(diff, 0.0) + lgb * jnp.maximum(-diff, 0.0))
    q_decay_f = jnp.exp(lgf * (row + 1.0))
    q_decay_b = jnp.exp(lgb * (C - row))
    pos = lax.broadcasted_iota(jnp.int32, (1, C), 1).astype(F32)
    k_decay_f = jnp.exp(lgf * (C - 1.0 - pos))
    chunk_decay_f = jnp.exp(jnp.full((1, RET_DV), lgf * C, F32))
    gn = gn_ref[...]

    state = state_ref[...]
    for c in range(n_c):
        tok = slice(c * C, (c + 1) * C)
        qc = q_ref[tok, :]
        kTc = kT_ref[:, tok]
        vc = v_ref[tok, :]
        qf32 = qc.astype(F32)
        scores = _dot(qc, kTc)
        inter = (_dot((qf32 * q_decay_f).astype(BF16), state.astype(BF16))
                 + _dot((qf32 * q_decay_b).astype(BF16), sb_ref[c]))
        k_dec = (kTc.astype(F32) * k_decay_f).astype(BF16)
        state = state * chunk_decay_f + _dot(k_dec, vc)
        o = inter + _dot((scores * intra).astype(BF16), vc)
        mu = jnp.mean(o, axis=-1, keepdims=True)
        d = o - mu
        var = jnp.mean(d * d, axis=-1, keepdims=True)
        y = d * lax.rsqrt(var + LN_EPS) * gn
        y_ref[tok, :] = (sg_ref[tok, :].astype(F32) * y).astype(y_ref.dtype)
    state_ref[...] = state


def _retention(q, kT, v, sg, sb, lgf, lgb, gn):
    B, S, _ = q.shape
    ts = min(RET_STEP, S)
    nS = S // ts
    n_c = ts // RET_CHUNK
    smem = pl.BlockSpec(memory_space=pltpu.SMEM)
    return pl.pallas_call(
        _ret_main_kernel,
        grid=(B, RET_HEADS, nS),
        in_specs=[
            smem, smem,
            pl.BlockSpec((None, ts, RET_DK), lambda b, h, i: (b, i, h)),
            pl.BlockSpec((None, RET_DK, ts), lambda b, h, i: (b, h, i)),
            pl.BlockSpec((None, ts, RET_DV), lambda b, h, i: (b, i, h)),
            pl.BlockSpec((None, ts, RET_DV), lambda b, h, i: (b, i, h)),
            pl.BlockSpec((None, None, n_c, RET_DK, RET_DV), lambda b, h, i: (b, h, i, 0, 0)),
            pl.BlockSpec((1, RET_DV), lambda b, h, i: (0, h)),
        ],
        out_specs=pl.BlockSpec((None, ts, RET_DV), lambda b, h, i: (b, i, h)),
        out_shape=jax.ShapeDtypeStruct((B, S, RET_V_WIDTH), BF16),
        scratch_shapes=[pltpu.VMEM((RET_DK, RET_DV), F32)],
        compiler_params=_params("parallel", "parallel", "arbitrary"),
        name="l1_retention",
    )(lgf, lgb, q, kT, v, sg, sb, gn.reshape(1, RET_V_WIDTH))


def _axial_rope_tables(seq_len, head_dim, sin_sign=None):
    rows = seq_len // GRID_W
    axis_dim = head_dim // 2
    inv_freq = ROPE_THETA ** (-jnp.arange(0, axis_dim, 2, dtype=F32) / axis_dim)
    freq = jnp.tile(inv_freq, LANES // inv_freq.shape[0])
    sign = 1.0 if sin_sign is None else sin_sign

    def table(n):
        ang = jnp.arange(n, dtype=F32)[:, None] * freq
        return jnp.stack([jnp.cos(ang), jnp.sin(ang) * sign])

    return table(rows), table(GRID_W)


def _rope_tile(row_tab, col_tab, head_dim):
    axis_dim = head_dim // 2
    lane = lax.broadcasted_iota(jnp.int32, col_tab.shape, 1)
    is_row = (lane % axis_dim) < (axis_dim // 2)
    pieces = [jnp.where(is_row, jnp.broadcast_to(row_tab[g:g + 1, :], col_tab.shape), col_tab)
              for g in range(row_tab.shape[0])]
    return pieces[0] if len(pieces) == 1 else jnp.concatenate(pieces, axis=0)


def kernel(x, l0_w_in, l0_q_norm_g, l0_k_norm_g, l0_dw_w, l0_dw_b, l0_conv_norm_g, l0_conv_norm_b,
           l0_w_out, l0_ln_mix_g, l0_ln_mix_b, l0_ffn_w_gate, l0_ffn_w_up, l0_ffn_w_down,
           l0_ln_ffn_g, l0_ln_ffn_b, l1_w_in, l1_log_decay_fwd, l1_log_decay_bwd, l1_ret_norm_g,
           l1_w_out, l1_ln_mix_g, l1_ln_mix_b, l1_ffn_w_gate, l1_ffn_w_up, l1_ffn_w_down,
           l1_ln_ffn_g, l1_ln_ffn_b):
    B, S, D = x.shape
    M = B * S
    bf = lambda w: w.astype(BF16)

    rotate_sign = jnp.where(jnp.arange(LANES) % HEAD_DIM < HEAD_DIM // 2, -1.0, 1.0).astype(F32)
    rope_a = _axial_rope_tables(S, HEAD_DIM, rotate_sign)
    rope_r = _axial_rope_tables(S, RET_DK)
    head_id = jnp.arange(ATTN_WIDTH) // HEAD_DIM
    grp = (head_id[:, None] == head_id[None, :]).astype(BF16)

    qT, k, vT, z = _l0_in_proj(
        x, bf(l0_w_in),
        jnp.tile(l0_q_norm_g, ATTN_HEADS).reshape(1, ATTN_WIDTH),
        jnp.tile(l0_k_norm_g, ATTN_KV_HEADS).reshape(1, KV_WIDTH),
        *rope_a, grp)
    attn = _attention(qT, k, vT)
    conv = _conformer_conv(z, l0_dw_w, l0_dw_b)
    x2d = x.reshape(M, D)
    x2d = _mix_out_ffn([attn.reshape(M, ATTN_WIDTH), conv.reshape(M, CONV_CH)], x2d, bf(l0_w_out),
                       l0_ln_mix_g, l0_ln_mix_b, bf(l0_ffn_w_gate), bf(l0_ffn_w_up), bf(l0_ffn_w_down),
                       l0_ln_ffn_g, l0_ln_ffn_b, last_part_norm=(l0_conv_norm_g, l0_conv_norm_b))

    q, kT, v, sg, sb = _l1_in_proj(x2d.reshape(B, S, D), bf(l1_w_in), *rope_r, l1_log_decay_bwd)
    y = _retention(q, kT, v, sg, sb, l1_log_decay_fwd, l1_log_decay_bwd, l1_ret_norm_g)
    x2d = _mix_out_ffn([y.reshape(M, RET_V_WIDTH)], x2d, bf(l1_w_out), l1_ln_mix_g, l1_ln_mix_b,
                       bf(l1_ffn_w_gate), bf(l1_ffn_w_up), bf(l1_ffn_w_down), l1_ln_ffn_g, l1_ln_ffn_b)
    return x2d.reshape(B, S, D)
```

```python
import functools

import jax
import jax.numpy as jnp
from jax import lax
from jax.experimental import pallas as pl
from jax.experimental.pallas import tpu as pltpu

F32 = jnp.float32
BF16 = jnp.bfloat16

GRID_W = 64
ROPE_THETA = 10000.0
ATTN_HEADS = 8
ATTN_KV_HEADS = 2
ATTN_GROUP = ATTN_HEADS // ATTN_KV_HEADS
HEAD_DIM = 64
ATTN_WIDTH = ATTN_HEADS * HEAD_DIM
KV_WIDTH = ATTN_KV_HEADS * HEAD_DIM
CONV_CH = 512
CONV_TAPS = 31
CONV_PAD = CONV_TAPS // 2
RET_HEADS = 4
RET_DK = 256
RET_DV = 512
RET_QK_WIDTH = RET_HEADS * RET_DK
RET_V_WIDTH = RET_HEADS * RET_DV
DEPTH = 2
DEEPNORM_ALPHA = (2 * DEPTH) ** 0.25
LN_EPS = 1e-5
RMS_EPS = 1e-6
LOG2_E = 1.4426950408889634

LANES = 128
SUBLANES = 8
BF16_ROWS = 16
VMEM_LIMIT_BYTES = 56 * 1024 * 1024

ROW_TILE = 512
ATTN_TQ = 256
ATTN_TK = 512
ATTN_SLOTS = 4
ATTN_HEADS_PER_BLOCK = 4
ATTN_UNROLL = 16
V_ROWS = HEAD_DIM + BF16_ROWS
CONV_HALO = 16
CONV_RB = 64
RET_CHUNK = 256
RET_STEP = 2048
FFN_HC = 256
MIX_ROW_GROUPS = 2


def _dot(a, b):
    return jnp.dot(a, b, preferred_element_type=F32)


def _layer_norm(y, g, b):
    mu = jnp.mean(y, axis=-1, keepdims=True)
    d = y - mu
    var = jnp.mean(d * d, axis=-1, keepdims=True)
    return d * lax.rsqrt(var + LN_EPS) * g + b


def _swish(t):
    return t * jax.nn.sigmoid(t)


def _params(*semantics, flags=None):
    return pltpu.CompilerParams(dimension_semantics=semantics, vmem_limit_bytes=VMEM_LIMIT_BYTES, flags=flags)


def _resident(shape):
    nd = len(shape)
    return pl.BlockSpec(shape, lambda *_: (0,) * nd, pipeline_mode=pl.Buffered(1))


def _l0_in_kernel(x_ref, w_ref, gq_ref, gk_ref, rope_row_ref, rope_col_ref, grp_ref,
                  qT_ref, k_ref, vT_ref, z_ref):
    tm = x_ref.shape[0]
    xb = x_ref[...].astype(BF16)
    cos = _rope_tile(rope_row_ref[0], rope_col_ref[0], HEAD_DIM)
    sin = _rope_tile(rope_row_ref[1], rope_col_ref[1], HEAD_DIM)
    lane = lax.broadcasted_iota(jnp.int32, (tm, LANES), 1)
    first_half = (lane % HEAD_DIM) < (HEAD_DIM // 2)

    def head_sum_sq(t, grp):
        t2 = t * t
        hi = t2.astype(BF16)
        lo = (t2 - hi.astype(F32)).astype(BF16)
        return _dot(hi, grp) + _dot(lo, grp)

    def norm_rope(t, ss, g):
        tn = t * lax.rsqrt(ss * (1.0 / HEAD_DIM) + RMS_EPS) * g
        outs = []
        for j in range(t.shape[1] // LANES):
            c = tn[:, j * LANES:(j + 1) * LANES]
            partner = jnp.where(first_half,
                                pltpu.roll(c, LANES - HEAD_DIM // 2, 1),
                                pltpu.roll(c, HEAD_DIM // 2, 1))
            outs.append(c * cos + partner * sin)
        return outs[0] if len(outs) == 1 else jnp.concatenate(outs, axis=1)

    c0 = 0
    q = _dot(xb, w_ref[:, c0:c0 + ATTN_WIDTH]); c0 += ATTN_WIDTH
    k = _dot(xb, w_ref[:, c0:c0 + KV_WIDTH]); c0 += KV_WIDTH
    v = _dot(xb, w_ref[:, c0:c0 + KV_WIDTH]); c0 += KV_WIDTH
    ss_q = head_sum_sq(q, grp_ref[...])
    ss_k = head_sum_sq(k, grp_ref[:KV_WIDTH, :KV_WIDTH])
    a = _dot(xb, w_ref[:, c0:c0 + CONV_CH]); c0 += CONV_CH
    gate = _dot(xb, w_ref[:, c0:c0 + CONV_CH])

    qr = norm_rope(q, ss_q, gq_ref[...]) * (HEAD_DIM ** -0.5 * LOG2_E)
    qT_ref[...] = qr.T.astype(BF16)
    k_ref[...] = norm_rope(k, ss_k, gk_ref[...]).astype(BF16)
    vT = v.T
    ones = jnp.ones((BF16_ROWS, tm), F32)
    for kk in range(ATTN_KV_HEADS):
        vT_ref[kk] = jnp.concatenate(
            [vT[kk * HEAD_DIM:(kk + 1) * HEAD_DIM], ones], axis=0).astype(BF16)
    z_ref[...] = a * jax.nn.sigmoid(gate)


def _l0_in_proj(x, w, gq, gk, rope_row, rope_col, grp):
    B, S, D = x.shape
    tm = min(ROW_TILE, S)
    nS = S // tm
    n_in = w.shape[1]
    assert tm % GRID_W == 0
    return pl.pallas_call(
        _l0_in_kernel,
        grid=(B, nS),
        in_specs=[
            pl.BlockSpec((None, tm, D), lambda b, i: (b, i, 0)),
            _resident((D, n_in)),
            _resident((1, ATTN_WIDTH)),
            _resident((1, KV_WIDTH)),
            pl.BlockSpec((2, tm // GRID_W, LANES), lambda b, i: (0, i, 0)),
            _resident(rope_col.shape),
            _resident((ATTN_WIDTH, ATTN_WIDTH)),
        ],
        out_specs=[
            pl.BlockSpec((None, ATTN_WIDTH, tm), lambda b, i: (b, 0, i)),
            pl.BlockSpec((None, tm, KV_WIDTH), lambda b, i: (b, i, 0)),
            pl.BlockSpec((None, ATTN_KV_HEADS, V_ROWS, tm), lambda b, i: (b, 0, 0, i)),
            pl.BlockSpec((None, tm, CONV_CH), lambda b, i: (b, i, 0)),
        ],
        out_shape=[
            jax.ShapeDtypeStruct((B, ATTN_WIDTH, S), BF16),
            jax.ShapeDtypeStruct((B, S, KV_WIDTH), BF16),
            jax.ShapeDtypeStruct((B, ATTN_KV_HEADS, V_ROWS, S), BF16),
            jax.ShapeDtypeStruct((B, S, CONV_CH), F32),
        ],
        compiler_params=_params("parallel", "parallel"),
        name="l0_in_proj",
    )(x, w, gq, gk, rope_row, rope_col, grp)


def _attn_kernel(zero_ref, qT_ref, k_ref, vT_ref, o_ref, *scratch, tk):
    U = ATTN_SLOTS
    s_bufs, p_bufs, oT_ref = scratch[:U], scratch[U:2 * U], scratch[2 * U]
    S = k_ref.shape[0]
    tq = qT_ref.shape[1]
    n_k = S // tk
    LA = U - 1
    unroll = ATTN_UNROLL
    assert n_k > LA and unroll % U == 0
    row_head = lax.broadcasted_iota(jnp.int32, (KV_WIDTH, tq), 0) // HEAD_DIM
    staged = pl.ds(pl.multiple_of(zero_ref[0], tk), tk)

    def one_head(h):
        kk = h // ATTN_GROUP
        row0 = pl.multiple_of(h * HEAD_DIM, HEAD_DIM)
        qh = qT_ref[pl.ds(row0, HEAD_DIM), :]
        q_ext = jnp.where(row_head == kk, jnp.concatenate([qh] * ATTN_KV_HEADS, axis=0), 0)

        def scores(c, slot):
            start = pl.multiple_of(c * tk, tk)
            s = _dot(k_ref[pl.ds(start, tk), :], q_ext).astype(BF16)
            s_bufs[slot][...] = s
            return jnp.max(s, axis=0, keepdims=True).astype(F32)

        def softmax(slot, m_run, m_chunk):
            m_new = jnp.maximum(m_run, m_chunk)
            p_bufs[slot][...] = jnp.exp2(s_bufs[slot][staged, :] - m_new.astype(BF16))
            return m_new, jnp.exp2(m_run - m_new)

        def values(c, slot, acc, alpha):
            start = pl.multiple_of(c * tk, tk)
            v_c = vT_ref[kk, :, pl.ds(start, tk)]
            return acc * alpha + _dot(v_c, p_bufs[slot][...])

        def substep(tau, r, carry, do_values=True, do_scores=True):
            m_run, m_chunks, alpha, acc = carry
            m_chunks = list(m_chunks)
            if do_values:
                acc = values(tau - 1, (r - 1) % U, acc, alpha)
            if do_scores:
                m_chunks[(r + LA) % U] = scores(tau + LA, (r + LA) % U)
            m_run, alpha = softmax(r, m_run, m_chunks[r])
            return m_run, tuple(m_chunks), alpha, acc

        neg_inf = jnp.full((1, tq), -jnp.inf, F32)
        m_chunks = [neg_inf] * U
        for c in range(LA):
            m_chunks[c] = scores(c, c)
        carry = (neg_inf, tuple(m_chunks), neg_inf, jnp.zeros((V_ROWS, tq), F32))
        carry = substep(0, 0, carry, do_values=False)

        def body(i, carry):
            tau0 = 1 + unroll * i
            for u in range(unroll):
                carry = substep(tau0 + u, (1 + u) % U, carry)
            return carry

        n_body = (n_k - LA - 1) // unroll
        carry = lax.fori_loop(0, n_body, body, carry)
        for tau in range(1 + unroll * n_body, n_k):
            carry = substep(tau, tau % U, carry, do_scores=tau + LA < n_k)
        _, _, alpha, acc = carry
        acc = values(n_k - 1, (n_k - 1) % U, acc, alpha)
        oT_ref[pl.ds(row0, HEAD_DIM), :] = acc[:HEAD_DIM] / acc[HEAD_DIM:HEAD_DIM + 1]

    def head_block(hb, carry_unused):
        for j in range(ATTN_HEADS_PER_BLOCK):
            one_head(hb * ATTN_HEADS_PER_BLOCK + j)
        return carry_unused

    lax.fori_loop(0, ATTN_HEADS // ATTN_HEADS_PER_BLOCK, head_block, 0)
    o_ref[...] = oT_ref[...].T.astype(o_ref.dtype)


def _attention(qT, k, vT):
    B, _, S = qT.shape
    tq = min(ATTN_TQ, S)
    tk = min(ATTN_TK, S // 4)
    return pl.pallas_call(
        functools.partial(_attn_kernel, tk=tk),
        grid=(B, S // tq),
        in_specs=[
            pl.BlockSpec(memory_space=pltpu.SMEM),
            pl.BlockSpec((None, ATTN_WIDTH, tq), lambda b, i: (b, 0, i)),
            pl.BlockSpec((None, S, KV_WIDTH), lambda b, i: (b, 0, 0)),
            pl.BlockSpec((None, ATTN_KV_HEADS, V_ROWS, S), lambda b, i: (b, 0, 0, 0)),
        ],
        out_specs=pl.BlockSpec((None, tq, ATTN_WIDTH), lambda b, i: (b, i, 0)),
        out_shape=jax.ShapeDtypeStruct((B, S, ATTN_WIDTH), BF16),
        scratch_shapes=(
            [pltpu.VMEM((tk, tq), BF16) for _ in range(2 * ATTN_SLOTS)]
            + [pltpu.VMEM((ATTN_WIDTH, tq), F32)]
        ),
        compiler_params=_params("parallel", "arbitrary"),
        name="l0_attention",
    )(jnp.zeros((1,), jnp.int32), qT, k, vT)


def _conv_kernel(zp_ref, zc_ref, zn_ref, w_ref, b_ref, o_ref, shift_ref):
    ts = zc_ref.shape[0]
    i = pl.program_id(1)
    n = pl.num_programs(1)
    ext = ts + 2 * CONV_HALO - SUBLANES
    prev = jnp.where(i > 0, zp_ref[...], 0.0)
    nxt = jnp.where(i < n - 1, zn_ref[...], 0.0)
    shift_ref[0, 0:CONV_HALO, :] = prev
    shift_ref[0, CONV_HALO:CONV_HALO + ts, :] = zc_ref[...]
    shift_ref[0, CONV_HALO + ts:, :] = nxt
    for r in range(1, SUBLANES):
        shift_ref[r, 0:ext, :] = shift_ref[0, r:r + ext, :]

    first = CONV_HALO - CONV_PAD
    n_grp = CONV_RB // SUBLANES
    a_max = (first + CONV_TAPS - 1) // SUBLANES

    def block(rb, carry):
        base = pl.multiple_of(rb * CONV_RB, CONV_RB)
        for lane0 in range(0, CONV_CH, LANES):
            lanes = slice(lane0, lane0 + LANES)
            acc = jnp.zeros((n_grp, SUBLANES, LANES), F32)
            for r in range(SUBLANES):
                rows = shift_ref[r, pl.ds(base, CONV_RB + a_max * SUBLANES), lanes]
                rows = rows.reshape(n_grp + a_max, SUBLANES, LANES)
                for a in range(a_max + 1):
                    j = a * SUBLANES + r - first
                    if 0 <= j < CONV_TAPS:
                        acc = acc + rows[a:a + n_grp] * w_ref[j, :, lanes]
            o_ref[pl.ds(base, CONV_RB), lanes] = acc.reshape(CONV_RB, LANES) + b_ref[:, lanes]
        return carry

    lax.fori_loop(0, ts // CONV_RB, block, 0)


def _conformer_conv(z, dw_w, dw_b):
    B, S, C = z.shape
    ts = min(ROW_TILE, S)
    nS = S // ts
    hb = ts // CONV_HALO
    last_halo = S // CONV_HALO - 1
    row = lambda v: v.reshape(1, C)
    return pl.pallas_call(
        _conv_kernel,
        grid=(B, nS),
        in_specs=[
            pl.BlockSpec((None, CONV_HALO, C), lambda b, i: (b, jnp.maximum(i * hb - 1, 0), 0)),
            pl.BlockSpec((None, ts, C), lambda b, i: (b, i, 0)),
            pl.BlockSpec((None, CONV_HALO, C), lambda b, i: (b, jnp.minimum((i + 1) * hb, last_halo), 0)),
            _resident((CONV_TAPS, SUBLANES, C)),
            _resident((1, C)),
        ],
        out_specs=pl.BlockSpec((None, ts, C), lambda b, i: (b, i, 0)),
        out_shape=jax.ShapeDtypeStruct((B, S, C), F32),
        scratch_shapes=[pltpu.VMEM((SUBLANES, ts + 2 * CONV_HALO, C), F32)],
        compiler_params=_params("parallel", "parallel"),
        name="l0_conv",
    )(z, z, z, jnp.broadcast_to(dw_w[:, None, :], (CONV_TAPS, SUBLANES, C)), row(dw_b))


def _out_proj_kernel(*refs, n_parts, norm_last):
    part_refs, rest = refs[:n_parts], refs[n_parts:]
    if norm_last:
        (ng_ref, nb_ref), rest = rest[:2], rest[2:]
    (x_ref, wo_ref, g1_ref, b1_ref, wg_ref, wu_ref, wd_ref, g2_ref, b2_ref, o_ref, h_ref) = rest
    tm = x_ref.shape[0]
    n_grp = MIX_ROW_GROUPS if tm % (MIX_ROW_GROUPS * SUBLANES * 2) == 0 else 1
    groups = [slice(r * (tm // n_grp), (r + 1) * (tm // n_grp)) for r in range(n_grp)]
    hidden = wg_ref.shape[1]

    outs = []
    for rows in groups:
        out = None
        row0 = 0
        for j, p_ref in enumerate(part_refs):
            width = p_ref.shape[1]
            part = p_ref[rows, :]
            if norm_last and j == n_parts - 1:
                part = _swish(_layer_norm(part, ng_ref[...], nb_ref[...])).astype(BF16)
            t = _dot(part, wo_ref[row0:row0 + width, :])
            out = t if out is None else out + t
            row0 += width
        outs.append(out)
    xs = [_layer_norm(DEEPNORM_ALPHA * x_ref[rows, :] + out, g1_ref[...], b1_ref[...])
          for rows, out in zip(groups, outs)]
    for rows, x in zip(groups, xs):
        xb = x.astype(BF16)
        for c in range(hidden // FFN_HC):
            cols = slice(c * FFN_HC, (c + 1) * FFN_HC)
            gate = _dot(xb, wg_ref[:, cols])
            up = _dot(xb, wu_ref[:, cols])
            h_ref[rows, cols] = (_swish(gate) * up).astype(BF16)
    for rows, x in zip(groups, xs):
        y = DEEPNORM_ALPHA * x + _dot(h_ref[rows, :], wd_ref[...])
        o_ref[rows, :] = _layer_norm(y, g2_ref[...], b2_ref[...])


def _mix_out_ffn(parts, x, wo, g1, b1, wg, wu, wd, g2, b2, last_part_norm=None):
    M, D = x.shape
    hidden = wg.shape[1]
    assert hidden % FFN_HC == 0
    tm = min(ROW_TILE, M)
    row = lambda v: v.reshape(1, -1)
    norm_args = [] if last_part_norm is None else [row(v) for v in last_part_norm]
    return pl.pallas_call(
        functools.partial(_out_proj_kernel, n_parts=len(parts), norm_last=last_part_norm is not None),
        grid=(M // tm,),
        in_specs=[pl.BlockSpec((tm, p.shape[1]), lambda i: (i, 0)) for p in parts]
        + [_resident(v.shape) for v in norm_args] + [
            pl.BlockSpec((tm, D), lambda i: (i, 0)),
            _resident(wo.shape), _resident((1, D)), _resident((1, D)),
            _resident(wg.shape), _resident(wu.shape), _resident(wd.shape),
            _resident((1, D)), _resident((1, D)),
        ],
        out_specs=pl.BlockSpec((tm, D), lambda i: (i, 0)),
        out_shape=jax.ShapeDtypeStruct((M, D), F32),
        scratch_shapes=[pltpu.VMEM((tm, hidden), BF16)],
        compiler_params=_params("parallel"),
        name="mix_out_ffn",
    )(*parts, *norm_args, x, wo, row(g1), row(b1), wg, wu, wd, row(g2), row(b2))


def _l1_in_kernel(lgb_ref, x_ref, w_ref, rope_row_ref, rope_col_ref, q_ref, kT_ref, v_ref, sg_ref, sb_ref,
                  state_ref, kt32_ref):
    xb = x_ref[...].astype(BF16)
    cos = _rope_tile(rope_row_ref[0], rope_col_ref[0], RET_DK)
    sin = _rope_tile(rope_row_ref[1], rope_col_ref[1], RET_DK)
    half = RET_DK // 2
    C = RET_CHUNK
    n_c = sb_ref.shape[1]

    @pl.when(pl.program_id(1) == 0)
    def _():
        state_ref[...] = jnp.zeros_like(state_ref)

    def rope_head(t):
        t1, t2 = t[:, :half], t[:, half:]
        return jnp.concatenate([t1 * cos - t2 * sin, t1 * sin + t2 * cos], axis=1)

    for h in range(RET_HEADS):
        c0 = RET_QK_WIDTH + h * RET_DK
        kT = (rope_head(_dot(xb, w_ref[:, c0:c0 + RET_DK])) * (RET_DK ** -0.5)).T
        kT_ref[h * RET_DK:(h + 1) * RET_DK, :] = kT.astype(BF16)
        kt32_ref[h * RET_DK:(h + 1) * RET_DK, :] = kT
    pos = lax.broadcasted_iota(jnp.int32, (1, C), 1).astype(F32)

    def project_v(h):
        c0 = 2 * RET_QK_WIDTH + h * RET_DV
        v_ref[:, h * RET_DV:(h + 1) * RET_DV] = _dot(xb, w_ref[:, c0:c0 + RET_DV]).astype(BF16)

    def project_q(h):
        cols = slice(h * RET_DK, (h + 1) * RET_DK)
        q_ref[:, cols] = rope_head(_dot(xb, w_ref[:, cols])).astype(BF16)

    def project_gate(h):
        c0 = 2 * RET_QK_WIDTH + RET_V_WIDTH + h * RET_DV
        sg_ref[:, h * RET_DV:(h + 1) * RET_DV] = _swish(_dot(xb, w_ref[:, c0:c0 + RET_DV])).astype(BF16)

    def update_state(h):
        lg = lgb_ref[h]
        k_decay = jnp.exp(lg * pos)
        chunk_decay = jnp.exp(jnp.full((1, RET_DV), lg * C, F32))
        rows = slice(h * RET_DK, (h + 1) * RET_DK)
        cols = slice(h * RET_DV, (h + 1) * RET_DV)
        updates = [_dot((kt32_ref[rows, c * C:(c + 1) * C] * k_decay).astype(BF16), v_ref[c * C:(c + 1) * C, cols])
                   for c in range(n_c)]
        state = state_ref[h]
        for c in reversed(range(n_c)):
            sb_ref[h, c] = state.astype(BF16)
            state = state * chunk_decay + updates[c]
        state_ref[h] = state

    project_v(0)
    for h in range(1, RET_HEADS):
        project_v(h)
        update_state(h - 1)
    project_q(0)
    update_state(RET_HEADS - 1)
    for h in range(1, RET_HEADS):
        project_q(h)
    for h in range(RET_HEADS):
        project_gate(h)


def _l1_in_proj(x, w, rope_row, rope_col, lgb):
    B, S, D = x.shape
    tm = min(ROW_TILE, S)
    nS = S // tm
    assert tm % GRID_W == 0
    n_c = tm // RET_CHUNK
    rev = lambda i: nS - 1 - i
    return pl.pallas_call(
        _l1_in_kernel,
        grid=(B, nS),
        in_specs=[
            pl.BlockSpec(memory_space=pltpu.SMEM),
            pl.BlockSpec((None, tm, D), lambda b, i: (b, rev(i), 0)),
            _resident(w.shape),
            pl.BlockSpec((2, tm // GRID_W, LANES), lambda b, i: (0, rev(i), 0)),
            _resident(rope_col.shape),
        ],
        out_specs=[
            pl.BlockSpec((None, tm, RET_QK_WIDTH), lambda b, i: (b, rev(i), 0)),
            pl.BlockSpec((None, RET_QK_WIDTH, tm), lambda b, i: (b, 0, rev(i))),
            pl.BlockSpec((None, tm, RET_V_WIDTH), lambda b, i: (b, rev(i), 0)),
            pl.BlockSpec((None, tm, RET_V_WIDTH), lambda b, i: (b, rev(i), 0)),
            pl.BlockSpec((None, RET_HEADS, n_c, RET_DK, RET_DV), lambda b, i: (b, 0, rev(i), 0, 0)),
        ],
        out_shape=[
            jax.ShapeDtypeStruct((B, S, RET_QK_WIDTH), BF16),
            jax.ShapeDtypeStruct((B, RET_QK_WIDTH, S), BF16),
            jax.ShapeDtypeStruct((B, S, RET_V_WIDTH), BF16),
            jax.ShapeDtypeStruct((B, S, RET_V_WIDTH), BF16),
            jax.ShapeDtypeStruct((B, RET_HEADS, S // RET_CHUNK, RET_DK, RET_DV), BF16),
        ],
        scratch_shapes=[pltpu.VMEM((RET_HEADS, RET_DK, RET_DV), F32), pltpu.VMEM((RET_QK_WIDTH, tm), F32)],
        compiler_params=_params("parallel", "arbitrary"),
        name="l1_in_proj",
    )(lgb, x, w, rope_row, rope_col)


def _ret_main_kernel(lgf_ref, lgb_ref, q_ref, kT_ref, v_ref, sg_ref, sb_ref, gn_ref, y_ref, state_ref):
    h = pl.program_id(1)
    n_c = sb_ref.shape[0]
    C = RET_CHUNK
    lgf = lgf_ref[h]
    lgb = lgb_ref[h]

    @pl.when(pl.program_id(2) == 0)
    def _():
        state_ref[...] = jnp.zeros_like(state_ref)

    row = lax.broadcasted_iota(jnp.int32, (C, C), 0).astype(F32)
    col = lax.broadcasted_iota(jnp.int32, (C, C), 1).astype(F32)
    diff = row - col
    intra = jnp.exp(lgf * jnp.maximum(diff, 0.0) + lgb * jnp.maximum(-diff, 0.0))
    q_decay_f = jnp.exp(lgf * (row + 1.0))
    q_decay_b = jnp.exp(lgb * (C - row))
    pos = lax.broadcasted_iota(jnp.int32, (1, C), 1).astype(F32)
    k_decay_f = jnp.exp(lgf * (C - 1.0 - pos))
    chunk_decay_f = jnp.exp(jnp.full((1, RET_DV), lgf * C, F32))
    gn = gn_ref[...]

    state = state_ref[...]
    for c in range(n_c):
        tok = slice(c * C, (c + 1) * C)
        qc = q_ref[tok, :]
        kTc = kT_ref[:, tok]
        vc = v_ref[tok, :]
        qf32 = qc.astype(F32)
        scores = _dot(qc, kTc)
        inter = (_dot((qf32 * q_decay_f).astype(BF16), state.astype(BF16))
                 + _dot((qf32 * q_decay_b).astype(BF16), sb_ref[c]))
        k_dec = (kTc.astype(F32) * k_decay_f).astype(BF16)
        state = state * chunk_decay_f + _dot(k_dec, vc)
        o = inter + _dot((scores * intra).astype(BF16), vc)
        mu = jnp.mean(o, axis=-1, keepdims=True)
        d = o - mu
        var = jnp.mean(d * d, axis=-1, keepdims=True)
        y = d * lax.rsqrt(var + LN_EPS) * gn
        y_ref[tok, :] = (sg_ref[tok, :].astype(F32) * y).astype(y_ref.dtype)
    state_ref[...] = state


def _retention(q, kT, v, sg, sb, lgf, lgb, gn):
    B, S, _ = q.shape
    ts = min(RET_STEP, S)
    nS = S // ts
    n_c = ts // RET_CHUNK
    smem = pl.BlockSpec(memory_space=pltpu.SMEM)
    return pl.pallas_call(
        _ret_main_kernel,
        grid=(B, RET_HEADS, nS),
        in_specs=[
            smem, smem,
            pl.BlockSpec((None, ts, RET_DK), lambda b, h, i: (b, i, h)),
            pl.BlockSpec((None, RET_DK, ts), lambda b, h, i: (b, h, i)),
            pl.BlockSpec((None, ts, RET_DV), lambda b, h, i: (b, i, h)),
            pl.BlockSpec((None, ts, RET_DV), lambda b, h, i: (b, i, h)),
            pl.BlockSpec((None, None, n_c, RET_DK, RET_DV), lambda b, h, i: (b, h, i, 0, 0)),
            pl.BlockSpec((1, RET_DV), lambda b, h, i: (0, h)),
        ],
        out_specs=pl.BlockSpec((None, ts, RET_DV), lambda b, h, i: (b, i, h)),
        out_shape=jax.ShapeDtypeStruct((B, S, RET_V_WIDTH), BF16),
        scratch_shapes=[pltpu.VMEM((RET_DK, RET_DV), F32)],
        compiler_params=_params("parallel", "parallel", "arbitrary"),
        name="l1_retention",
    )(lgf, lgb, q, kT, v, sg, sb, gn.reshape(1, RET_V_WIDTH))


def _axial_rope_tables(seq_len, head_dim, sin_sign=None):
    rows = seq_len // GRID_W
    axis_dim = head_dim // 2
    inv_freq = ROPE_THETA ** (-jnp.arange(0, axis_dim, 2, dtype=F32) / axis_dim)
    freq = jnp.tile(inv_freq, LANES // inv_freq.shape[0])
    sign = 1.0 if sin_sign is None else sin_sign

    def table(n):
        ang = jnp.arange(n, dtype=F32)[:, None] * freq
        return jnp.stack([jnp.cos(ang), jnp.sin(ang) * sign])

    return table(rows), table(GRID_W)


def _rope_tile(row_tab, col_tab, head_dim):
    axis_dim = head_dim // 2
    lane = lax.broadcasted_iota(jnp.int32, col_tab.shape, 1)
    is_row = (lane % axis_dim) < (axis_dim // 2)
    pieces = [jnp.where(is_row, jnp.broadcast_to(row_tab[g:g + 1, :], col_tab.shape), col_tab)
              for g in range(row_tab.shape[0])]
    return pieces[0] if len(pieces) == 1 else jnp.concatenate(pieces, axis=0)


def kernel(x, l0_w_in, l0_q_norm_g, l0_k_norm_g, l0_dw_w, l0_dw_b, l0_conv_norm_g, l0_conv_norm_b,
           l0_w_out, l0_ln_mix_g, l0_ln_mix_b, l0_ffn_w_gate, l0_ffn_w_up, l0_ffn_w_down,
           l0_ln_ffn_g, l0_ln_ffn_b, l1_w_in, l1_log_decay_fwd, l1_log_decay_bwd, l1_ret_norm_g,
           l1_w_out, l1_ln_mix_g, l1_ln_mix_b, l1_ffn_w_gate, l1_ffn_w_up, l1_ffn_w_down,
           l1_ln_ffn_g, l1_ln_ffn_b):
    B, S, D = x.shape
    M = B * S
    bf = lambda w: w.astype(BF16)

    rotate_sign = jnp.where(jnp.arange(LANES) % HEAD_DIM < HEAD_DIM // 2, -1.0, 1.0).astype(F32)
    rope_a = _axial_rope_tables(S, HEAD_DIM, rotate_sign)
    rope_r = _axial_rope_tables(S, RET_DK)
    head_id = jnp.arange(ATTN_WIDTH) // HEAD_DIM
    grp = (head_id[:, None] == head_id[None, :]).astype(BF16)

    qT, k, vT, z = _l0_in_proj(
        x, bf(l0_w_in),
        jnp.tile(l0_q_norm_g, ATTN_HEADS).reshape(1, ATTN_WIDTH),
        jnp.tile(l0_k_norm_g, ATTN_KV_HEADS).reshape(1, KV_WIDTH),
        *rope_a, grp)
    attn = _attention(qT, k, vT)
    conv = _conformer_conv(z, l0_dw_w, l0_dw_b)
    x2d = x.reshape(M, D)
    x2d = _mix_out_ffn([attn.reshape(M, ATTN_WIDTH), conv.reshape(M, CONV_CH)], x2d, bf(l0_w_out),
                       l0_ln_mix_g, l0_ln_mix_b, bf(l0_ffn_w_gate), bf(l0_ffn_w_up), bf(l0_ffn_w_down),
                       l0_ln_ffn_g, l0_ln_ffn_b, last_part_norm=(l0_conv_norm_g, l0_conv_norm_b))

    q, kT, v, sg, sb = _l1_in_proj(x2d.reshape(B, S, D), bf(l1_w_in), *rope_r, l1_log_decay_bwd)
    y = _retention(q, kT, v, sg, sb, l1_log_decay_fwd, l1_log_decay_bwd, l1_ret_norm_g)
    x2d = _mix_out_ffn([y.reshape(M, RET_V_WIDTH)], x2d, bf(l1_w_out), l1_ln_mix_g, l1_ln_mix_b,
                       bf(l1_ffn_w_gate), bf(l1_ffn_w_up), bf(l1_ffn_w_down), l1_ln_ffn_g, l1_ln_ffn_b)
    return x2d.reshape(B, S, D)
```

```python
import functools

import jax
import jax.numpy as jnp
from jax import lax
from jax.experimental import pallas as pl
from jax.experimental.pallas import tpu as pltpu

F32 = jnp.float32
BF16 = jnp.bfloat16

GRID_W = 64
ROPE_THETA = 10000.0
ATTN_HEADS = 8
ATTN_KV_HEADS = 2
ATTN_GROUP = ATTN_HEADS // ATTN_KV_HEADS
HEAD_DIM = 64
ATTN_WIDTH = ATTN_HEADS * HEAD_DIM
KV_WIDTH = ATTN_KV_HEADS * HEAD_DIM
CONV_CH = 512
CONV_TAPS = 31
CONV_PAD = CONV_TAPS // 2
RET_HEADS = 4
RET_DK = 256
RET_DV = 512
RET_QK_WIDTH = RET_HEADS * RET_DK
RET_V_WIDTH = RET_HEADS * RET_DV
DEPTH = 2
DEEPNORM_ALPHA = (2 * DEPTH) ** 0.25
LN_EPS = 1e-5
RMS_EPS = 1e-6
LOG2_E = 1.4426950408889634

LANES = 128
SUBLANES = 8
BF16_ROWS = 16
VMEM_LIMIT_BYTES = 56 * 1024 * 1024

ROW_TILE = 512
ATTN_TQ = 256
ATTN_TK = 512
ATTN_SLOTS = 4
ATTN_HEADS_PER_BLOCK = 4
ATTN_UNROLL = 16
V_ROWS = HEAD_DIM + BF16_ROWS
CONV_HALO = 16
CONV_RB = 128
RET_CHUNK = 256
RET_STEP = 2048
FFN_HC = 256
MIX_ROW_GROUPS = 2


def _dot(a, b):
    return jnp.dot(a, b, preferred_element_type=F32)


def _layer_norm(y, g, b):
    mu = jnp.mean(y, axis=-1, keepdims=True)
    d = y - mu
    var = jnp.mean(d * d, axis=-1, keepdims=True)
    return d * lax.rsqrt(var + LN_EPS) * g + b


def _swish(t):
    return t * jax.nn.sigmoid(t)


def _params(*semantics, flags=None):
    return pltpu.CompilerParams(dimension_semantics=semantics, vmem_limit_bytes=VMEM_LIMIT_BYTES, flags=flags)


def _resident(shape):
    nd = len(shape)
    return pl.BlockSpec(shape, lambda *_: (0,) * nd, pipeline_mode=pl.Buffered(1))


def _l0_in_kernel(x_ref, w_ref, gq_ref, gk_ref, rope_row_ref, rope_col_ref, grp_ref,
                  qT_ref, k_ref, vT_ref, z_ref):
    tm = x_ref.shape[0]
    xb = x_ref[...].astype(BF16)
    cos = _rope_tile(rope_row_ref[0], rope_col_ref[0], HEAD_DIM)
    sin = _rope_tile(rope_row_ref[1], rope_col_ref[1], HEAD_DIM)
    lane = lax.broadcasted_iota(jnp.int32, (tm, LANES), 1)
    first_half = (lane % HEAD_DIM) < (HEAD_DIM // 2)

    def head_sum_sq(t, grp):
        t2 = t * t
        hi = t2.astype(BF16)
        lo = (t2 - hi.astype(F32)).astype(BF16)
        return _dot(hi, grp) + _dot(lo, grp)

    def norm_rope(t, ss, g):
        tn = t * lax.rsqrt(ss * (1.0 / HEAD_DIM) + RMS_EPS) * g
        outs = []
        for j in range(t.shape[1] // LANES):
            c = tn[:, j * LANES:(j + 1) * LANES]
            partner = jnp.where(first_half,
                                pltpu.roll(c, LANES - HEAD_DIM // 2, 1),
                                pltpu.roll(c, HEAD_DIM // 2, 1))
            outs.append(c * cos + partner * sin)
        return outs[0] if len(outs) == 1 else jnp.concatenate(outs, axis=1)

    c0 = 0
    q = _dot(xb, w_ref[:, c0:c0 + ATTN_WIDTH]); c0 += ATTN_WIDTH
    k = _dot(xb, w_ref[:, c0:c0 + KV_WIDTH]); c0 += KV_WIDTH
    v = _dot(xb, w_ref[:, c0:c0 + KV_WIDTH]); c0 += KV_WIDTH
    ss_q = head_sum_sq(q, grp_ref[...])
    ss_k = head_sum_sq(k, grp_ref[:KV_WIDTH, :KV_WIDTH])
    a = _dot(xb, w_ref[:, c0:c0 + CONV_CH]); c0 += CONV_CH
    gate = _dot(xb, w_ref[:, c0:c0 + CONV_CH])

    qr = norm_rope(q, ss_q, gq_ref[...]) * (HEAD_DIM ** -0.5 * LOG2_E)
    qT_ref[...] = qr.T.astype(BF16)
    k_ref[...] = norm_rope(k, ss_k, gk_ref[...]).astype(BF16)
    vT = v.T
    ones = jnp.ones((BF16_ROWS, tm), F32)
    for kk in range(ATTN_KV_HEADS):
        vT_ref[kk] = jnp.concatenate(
            [vT[kk * HEAD_DIM:(kk + 1) * HEAD_DIM], ones], axis=0).astype(BF16)
    z_ref[...] = a * jax.nn.sigmoid(gate)


def _l0_in_proj(x, w, gq, gk, rope_row, rope_col, grp):
    B, S, D = x.shape
    tm = min(ROW_TILE, S)
    nS = S // tm
    n_in = w.shape[1]
    assert tm % GRID_W == 0
    return pl.pallas_call(
        _l0_in_kernel,
        grid=(B, nS),
        in_specs=[
            pl.BlockSpec((None, tm, D), lambda b, i: (b, i, 0)),
            _resident((D, n_in)),
            _resident((1, ATTN_WIDTH)),
            _resident((1, KV_WIDTH)),
            pl.BlockSpec((2, tm // GRID_W, LANES), lambda b, i: (0, i, 0)),
            _resident(rope_col.shape),
            _resident((ATTN_WIDTH, ATTN_WIDTH)),
        ],
        out_specs=[
            pl.BlockSpec((None, ATTN_WIDTH, tm), lambda b, i: (b, 0, i)),
            pl.BlockSpec((None, tm, KV_WIDTH), lambda b, i: (b, i, 0)),
            pl.BlockSpec((None, ATTN_KV_HEADS, V_ROWS, tm), lambda b, i: (b, 0, 0, i)),
            pl.BlockSpec((None, tm, CONV_CH), lambda b, i: (b, i, 0)),
        ],
        out_shape=[
            jax.ShapeDtypeStruct((B, ATTN_WIDTH, S), BF16),
            jax.ShapeDtypeStruct((B, S, KV_WIDTH), BF16),
            jax.ShapeDtypeStruct((B, ATTN_KV_HEADS, V_ROWS, S), BF16),
            jax.ShapeDtypeStruct((B, S, CONV_CH), F32),
        ],
        compiler_params=_params("parallel", "parallel"),
        name="l0_in_proj",
    )(x, w, gq, gk, rope_row, rope_col, grp)


def _attn_kernel(zero_ref, qT_ref, k_ref, vT_ref, o_ref, *scratch, tk):
    U = ATTN_SLOTS
    s_bufs, p_bufs, oT_ref = scratch[:U], scratch[U:2 * U], scratch[2 * U]
    S = k_ref.shape[0]
    tq = qT_ref.shape[1]
    n_k = S // tk
    LA = U - 1
    unroll = ATTN_UNROLL
    assert n_k > LA and unroll % U == 0
    row_head = lax.broadcasted_iota(jnp.int32, (KV_WIDTH, tq), 0) // HEAD_DIM
    staged = pl.ds(pl.multiple_of(zero_ref[0], tk), tk)

    def one_head(h):
        kk = h // ATTN_GROUP
        row0 = pl.multiple_of(h * HEAD_DIM, HEAD_DIM)
        qh = qT_ref[pl.ds(row0, HEAD_DIM), :]
        q_ext = jnp.where(row_head == kk, jnp.concatenate([qh] * ATTN_KV_HEADS, axis=0), 0)

        def scores(c, slot):
            start = pl.multiple_of(c * tk, tk)
            s = _dot(k_ref[pl.ds(start, tk), :], q_ext).astype(BF16)
            s_bufs[slot][...] = s
            return jnp.max(s, axis=0, keepdims=True).astype(F32)

        def softmax(slot, m_run, m_chunk):
            m_new = jnp.maximum(m_run, m_chunk)
            p_bufs[slot][...] = jnp.exp2(s_bufs[slot][staged, :] - m_new.astype(BF16))
            return m_new, jnp.exp2(m_run - m_new)

        def values(c, slot, acc, alpha):
            start = pl.multiple_of(c * tk, tk)
            v_c = vT_ref[kk, :, pl.ds(start, tk)]
            return acc * alpha + _dot(v_c, p_bufs[slot][...])

        def substep(tau, r, carry, do_values=True, do_scores=True):
            m_run, m_chunks, alpha, acc = carry
            m_chunks = list(m_chunks)
            if do_values:
                acc = values(tau - 1, (r - 1) % U, acc, alpha)
            if do_scores:
                m_chunks[(r + LA) % U] = scores(tau + LA, (r + LA) % U)
            m_run, alpha = softmax(r, m_run, m_chunks[r])
            return m_run, tuple(m_chunks), alpha, acc

        neg_inf = jnp.full((1, tq), -jnp.inf, F32)
        m_chunks = [neg_inf] * U
        for c in range(LA):
            m_chunks[c] = scores(c, c)
        carry = (neg_inf, tuple(m_chunks), neg_inf, jnp.zeros((V_ROWS, tq), F32))
        carry = substep(0, 0, carry, do_values=False)

        def body(i, carry):
            tau0 = 1 + unroll * i
            for u in range(unroll):
                carry = substep(tau0 + u, (1 + u) % U, carry)
            return carry

        n_body = (n_k - LA - 1) // unroll
        carry = lax.fori_loop(0, n_body, body, carry)
        for tau in range(1 + unroll * n_body, n_k):
            carry = substep(tau, tau % U, carry, do_scores=tau + LA < n_k)
        _, _, alpha, acc = carry
        acc = values(n_k - 1, (n_k - 1) % U, acc, alpha)
        oT_ref[pl.ds(row0, HEAD_DIM), :] = acc[:HEAD_DIM] / acc[HEAD_DIM:HEAD_DIM + 1]

    def head_block(hb, carry_unused):
        for j in range(ATTN_HEADS_PER_BLOCK):
            one_head(hb * ATTN_HEADS_PER_BLOCK + j)
        return carry_unused

    lax.fori_loop(0, ATTN_HEADS // ATTN_HEADS_PER_BLOCK, head_block, 0)
    o_ref[...] = oT_ref[...].T.astype(o_ref.dtype)


def _attention(qT, k, vT):
    B, _, S = qT.shape
    tq = min(ATTN_TQ, S)
    tk = min(ATTN_TK, S // 4)
    return pl.pallas_call(
        functools.partial(_attn_kernel, tk=tk),
        grid=(B, S // tq),
        in_specs=[
            pl.BlockSpec(memory_space=pltpu.SMEM),
            pl.BlockSpec((None, ATTN_WIDTH, tq), lambda b, i: (b, 0, i)),
            pl.BlockSpec((None, S, KV_WIDTH), lambda b, i: (b, 0, 0)),
            pl.BlockSpec((None, ATTN_KV_HEADS, V_ROWS, S), lambda b, i: (b, 0, 0, 0)),
        ],
        out_specs=pl.BlockSpec((None, tq, ATTN_WIDTH), lambda b, i: (b, i, 0)),
        out_shape=jax.ShapeDtypeStruct((B, S, ATTN_WIDTH), BF16),
        scratch_shapes=(
            [pltpu.VMEM((tk, tq), BF16) for _ in range(2 * ATTN_SLOTS)]
            + [pltpu.VMEM((ATTN_WIDTH, tq), F32)]
        ),
        compiler_params=_params("parallel", "arbitrary"),
        name="l0_attention",
    )(jnp.zeros((1,), jnp.int32), qT, k, vT)


def _conv_kernel(zp_ref, zc_ref, zn_ref, w_ref, b_ref, o_ref, shift_ref):
    ts = zc_ref.shape[0]
    i = pl.program_id(1)
    n = pl.num_programs(1)
    ext = ts + 2 * CONV_HALO - SUBLANES
    prev = jnp.where(i > 0, zp_ref[...], 0.0)
    nxt = jnp.where(i < n - 1, zn_ref[...], 0.0)
    shift_ref[0, 0:CONV_HALO, :] = prev
    shift_ref[0, CONV_HALO:CONV_HALO + ts, :] = zc_ref[...]
    shift_ref[0, CONV_HALO + ts:, :] = nxt
    for r in range(1, SUBLANES):
        shift_ref[r, 0:ext, :] = shift_ref[0, r:r + ext, :]

    first = CONV_HALO - CONV_PAD
    n_grp = CONV_RB // SUBLANES
    a_max = (first + CONV_TAPS - 1) // SUBLANES

    def block(rb, carry):
        base = pl.multiple_of(rb * CONV_RB, CONV_RB)
        for lane0 in range(0, CONV_CH, LANES):
            lanes = slice(lane0, lane0 + LANES)
            acc = jnp.zeros((n_grp, SUBLANES, LANES), F32)
            for r in range(SUBLANES):
                rows = shift_ref[r, pl.ds(base, CONV_RB + a_max * SUBLANES), lanes]
                rows = rows.reshape(n_grp + a_max, SUBLANES, LANES)
                for a in range(a_max + 1):
                    j = a * SUBLANES + r - first
                    if 0 <= j < CONV_TAPS:
                        acc = acc + rows[a:a + n_grp] * w_ref[j, :, lanes]
            o_ref[pl.ds(base, CONV_RB), lanes] = acc.reshape(CONV_RB, LANES) + b_ref[:, lanes]
        return carry

    lax.fori_loop(0, ts // CONV_RB, block, 0)


def _conformer_conv(z, dw_w, dw_b):
    B, S, C = z.shape
    ts = min(ROW_TILE, S)
    nS = S // ts
    hb = ts // CONV_HALO
    last_halo = S // CONV_HALO - 1
    row = lambda v: v.reshape(1, C)
    return pl.pallas_call(
        _conv_kernel,
        grid=(B, nS),
        in_specs=[
            pl.BlockSpec((None, CONV_HALO, C), lambda b, i: (b, jnp.maximum(i * hb - 1, 0), 0)),
            pl.BlockSpec((None, ts, C), lambda b, i: (b, i, 0)),
            pl.BlockSpec((None, CONV_HALO, C), lambda b, i: (b, jnp.minimum((i + 1) * hb, last_halo), 0)),
            _resident((CONV_TAPS, SUBLANES, C)),
            _resident((1, C)),
        ],
        out_specs=pl.BlockSpec((None, ts, C), lambda b, i: (b, i, 0)),
        out_shape=jax.ShapeDtypeStruct((B, S, C), F32),
        scratch_shapes=[pltpu.VMEM((SUBLANES, ts + 2 * CONV_HALO, C), F32)],
        compiler_params=_params("parallel", "parallel"),
        name="l0_conv",
    )(z, z, z, jnp.broadcast_to(dw_w[:, None, :], (CONV_TAPS, SUBLANES, C)), row(dw_b))


def _out_proj_kernel(*refs, n_parts, norm_last):
    part_refs, rest = refs[:n_parts], refs[n_parts:]
    if norm_last:
        (ng_ref, nb_ref), rest = rest[:2], rest[2:]
    (x_ref, wo_ref, g1_ref, b1_ref, wg_ref, wu_ref, wd_ref, g2_ref, b2_ref, o_ref, h_ref) = rest
    tm = x_ref.shape[0]
    n_grp = MIX_ROW_GROUPS if tm % (MIX_ROW_GROUPS * SUBLANES * 2) == 0 else 1
    groups = [slice(r * (tm // n_grp), (r + 1) * (tm // n_grp)) for r in range(n_grp)]
    hidden = wg_ref.shape[1]

    outs = []
    for rows in groups:
        out = None
        row0 = 0
        for j, p_ref in enumerate(part_refs):
            width = p_ref.shape[1]
            part = p_ref[rows, :]
            if norm_last and j == n_parts - 1:
                part = _swish(_layer_norm(part, ng_ref[...], nb_ref[...])).astype(BF16)
            t = _dot(part, wo_ref[row0:row0 + width, :])
            out = t if out is None else out + t
            row0 += width
        outs.append(out)
    xs = [_layer_norm(DEEPNORM_ALPHA * x_ref[rows, :] + out, g1_ref[...], b1_ref[...])
          for rows, out in zip(groups, outs)]
    for rows, x in zip(groups, xs):
        xb = x.astype(BF16)
        for c in range(hidden // FFN_HC):
            cols = slice(c * FFN_HC, (c + 1) * FFN_HC)
            gate = _dot(xb, wg_ref[:, cols])
            up = _dot(xb, wu_ref[:, cols])
            h_ref[rows, cols] = (_swish(gate) * up).astype(BF16)
    for rows, x in zip(groups, xs):
        y = DEEPNORM_ALPHA * x + _dot(h_ref[rows, :], wd_ref[...])
        o_ref[rows, :] = _layer_norm(y, g2_ref[...], b2_ref[...])


def _mix_out_ffn(parts, x, wo, g1, b1, wg, wu, wd, g2, b2, last_part_norm=None):
    M, D = x.shape
    hidden = wg.shape[1]
    assert hidden % FFN_HC == 0
    tm = min(ROW_TILE, M)
    row = lambda v: v.reshape(1, -1)
    norm_args = [] if last_part_norm is None else [row(v) for v in last_part_norm]
    return pl.pallas_call(
        functools.partial(_out_proj_kernel, n_parts=len(parts), norm_last=last_part_norm is not None),
        grid=(M // tm,),
        in_specs=[pl.BlockSpec((tm, p.shape[1]), lambda i: (i, 0)) for p in parts]
        + [_resident(v.shape) for v in norm_args] + [
            pl.BlockSpec((tm, D), lambda i: (i, 0)),
            _resident(wo.shape), _resident((1, D)), _resident((1, D)),
            _resident(wg.shape), _resident(wu.shape), _resident(wd.shape),
            _resident((1, D)), _resident((1, D)),
        ],
        out_specs=pl.BlockSpec((tm, D), lambda i: (i, 0)),
        out_shape=jax.ShapeDtypeStruct((M, D), F32),
        scratch_shapes=[pltpu.VMEM((tm, hidden), BF16)],
        compiler_params=_params("parallel"),
        name="mix_out_ffn",
    )(*parts, *norm_args, x, wo, row(g1), row(b1), wg, wu, wd, row(g2), row(b2))


def _l1_in_kernel(lgb_ref, x_ref, w_ref, rope_row_ref, rope_col_ref, q_ref, kT_ref, v_ref, sg_ref, sb_ref,
                  state_ref, kt32_ref):
    xb = x_ref[...].astype(BF16)
    cos = _rope_tile(rope_row_ref[0], rope_col_ref[0], RET_DK)
    sin = _rope_tile(rope_row_ref[1], rope_col_ref[1], RET_DK)
    half = RET_DK // 2
    C = RET_CHUNK
    n_c = sb_ref.shape[1]

    @pl.when(pl.program_id(1) == 0)
    def _():
        state_ref[...] = jnp.zeros_like(state_ref)

    def rope_head(t):
        t1, t2 = t[:, :half], t[:, half:]
        return jnp.concatenate([t1 * cos - t2 * sin, t1 * sin + t2 * cos], axis=1)

    for h in range(RET_HEADS):
        c0 = RET_QK_WIDTH + h * RET_DK
        kT = (rope_head(_dot(xb, w_ref[:, c0:c0 + RET_DK])) * (RET_DK ** -0.5)).T
        kT_ref[h * RET_DK:(h + 1) * RET_DK, :] = kT.astype(BF16)
        kt32_ref[h * RET_DK:(h + 1) * RET_DK, :] = kT
    pos = lax.broadcasted_iota(jnp.int32, (1, C), 1).astype(F32)

    def project_v(h):
        c0 = 2 * RET_QK_WIDTH + h * RET_DV
        v_ref[:, h * RET_DV:(h + 1) * RET_DV] = _dot(xb, w_ref[:, c0:c0 + RET_DV]).astype(BF16)

    def project_q(h):
        cols = slice(h * RET_DK, (h + 1) * RET_DK)
        q_ref[:, cols] = rope_head(_dot(xb, w_ref[:, cols])).astype(BF16)

    def project_gate(h):
        c0 = 2 * RET_QK_WIDTH + RET_V_WIDTH + h * RET_DV
        sg_ref[:, h * RET_DV:(h + 1) * RET_DV] = _swish(_dot(xb, w_ref[:, c0:c0 + RET_DV])).astype(BF16)

    def update_state(h):
        lg = lgb_ref[h]
        k_decay = jnp.exp(lg * pos)
        chunk_decay = jnp.exp(jnp.full((1, RET_DV), lg * C, F32))
        rows = slice(h * RET_DK, (h + 1) * RET_DK)
        cols = slice(h * RET_DV, (h + 1) * RET_DV)
        updates = [_dot((kt32_ref[rows, c * C:(c + 1) * C] * k_decay).astype(BF16), v_ref[c * C:(c + 1) * C, cols])
                   for c in range(n_c)]
        state = state_ref[h]
        for c in reversed(range(n_c)):
            sb_ref[h, c] = state.astype(BF16)
            state = state * chunk_decay + updates[c]
        state_ref[h] = state

    project_v(0)
    for h in range(1, RET_HEADS):
        project_v(h)
        update_state(h - 1)
    project_q(0)
    update_state(RET_HEADS - 1)
    for h in range(1, RET_HEADS):
        project_q(h)
    for h in range(RET_HEADS):
        project_gate(h)


def _l1_in_proj(x, w, rope_row, rope_col, lgb):
    B, S, D = x.shape
    tm = min(ROW_TILE, S)
    nS = S // tm
    assert tm % GRID_W == 0
    n_c = tm // RET_CHUNK
    rev = lambda i: nS - 1 - i
    return pl.pallas_call(
        _l1_in_kernel,
        grid=(B, nS),
        in_specs=[
            pl.BlockSpec(memory_space=pltpu.SMEM),
            pl.BlockSpec((None, tm, D), lambda b, i: (b, rev(i), 0)),
            _resident(w.shape),
            pl.BlockSpec((2, tm // GRID_W, LANES), lambda b, i: (0, rev(i), 0)),
            _resident(rope_col.shape),
        ],
        out_specs=[
            pl.BlockSpec((None, tm, RET_QK_WIDTH), lambda b, i: (b, rev(i), 0)),
            pl.BlockSpec((None, RET_QK_WIDTH, tm), lambda b, i: (b, 0, rev(i))),
            pl.BlockSpec((None, tm, RET_V_WIDTH), lambda b, i: (b, rev(i), 0)),
            pl.BlockSpec((None, tm, RET_V_WIDTH), lambda b, i: (b, rev(i), 0)),
            pl.BlockSpec((None, RET_HEADS, n_c, RET_DK, RET_DV), lambda b, i: (b, 0, rev(i), 0, 0)),
        ],
        out_shape=[
            jax.ShapeDtypeStruct((B, S, RET_QK_WIDTH), BF16),
            jax.ShapeDtypeStruct((B, RET_QK_WIDTH, S), BF16),
            jax.ShapeDtypeStruct((B, S, RET_V_WIDTH), BF16),
            jax.ShapeDtypeStruct((B, S, RET_V_WIDTH), BF16),
            jax.ShapeDtypeStruct((B, RET_HEADS, S // RET_CHUNK, RET_DK, RET_DV), BF16),
        ],
        scratch_shapes=[pltpu.VMEM((RET_HEADS, RET_DK, RET_DV), F32), pltpu.VMEM((RET_QK_WIDTH, tm), F32)],
        compiler_params=_params("parallel", "arbitrary"),
        name="l1_in_proj",
    )(lgb, x, w, rope_row, rope_col)


def _ret_main_kernel(lgf_ref, lgb_ref, q_ref, kT_ref, v_ref, sg_ref, sb_ref, gn_ref, y_ref, state_ref):
    h = pl.program_id(1)
    n_c = sb_ref.shape[0]
    C = RET_CHUNK
    lgf = lgf_ref[h]
    lgb = lgb_ref[h]

    @pl.when(pl.program_id(2) == 0)
    def _():
        state_ref[...] = jnp.zeros_like(state_ref)

    row = lax.broadcasted_iota(jnp.int32, (C, C), 0).astype(F32)
    col = lax.broadcasted_iota(jnp.int32, (C, C), 1).astype(F32)
    diff = row - col
    intra = jnp.exp(lgf * jnp.maximum(diff, 0.0) + lgb * jnp.maximum(-diff, 0.0))
    q_decay_f = jnp.exp(lgf * (row + 1.0))
    q_decay_b = jnp.exp(lgb * (C - row))
    pos = lax.broadcasted_iota(jnp.int32, (1, C), 1).astype(F32)
    k_decay_f = jnp.exp(lgf * (C - 1.0 - pos))
    chunk_decay_f = jnp.exp(jnp.full((1, RET_DV), lgf * C, F32))
    gn = gn_ref[...]

    state = state_ref[...]
    for c in range(n_c):
        tok = slice(c * C, (c + 1) * C)
        qc = q_ref[tok, :]
        kTc = kT_ref[:, tok]
        vc = v_ref[tok, :]
        qf32 = qc.astype(F32)
        scores = _dot(qc, kTc)
        inter = (_dot((qf32 * q_decay_f).astype(BF16), state.astype(BF16))
                 + _dot((qf32 * q_decay_b).astype(BF16), sb_ref[c]))
        k_dec = (kTc.astype(F32) * k_decay_f).astype(BF16)
        state = state * chunk_decay_f + _dot(k_dec, vc)
        o = inter + _dot((scores * intra).astype(BF16), vc)
        mu = jnp.mean(o, axis=-1, keepdims=True)
        d = o - mu
        var = jnp.mean(d * d, axis=-1, keepdims=True)
        y = d * lax.rsqrt(var + LN_EPS) * gn
        y_ref[tok, :] = (sg_ref[tok, :].astype(F32) * y).astype(y_ref.dtype)
    state_ref[...] = state


def _retention(q, kT, v, sg, sb, lgf, lgb, gn):
    B, S, _ = q.shape
    ts = min(RET_STEP, S)
    nS = S // ts
    n_c = ts // RET_CHUNK
    smem = pl.BlockSpec(memory_space=pltpu.SMEM)
    return pl.pallas_call(
        _ret_main_kernel,
        grid=(B, RET_HEADS, nS),
        in_specs=[
            smem, smem,
            pl.BlockSpec((None, ts, RET_DK), lambda b, h, i: (b, i, h)),
            pl.BlockSpec((None, RET_DK, ts), lambda b, h, i: (b, h, i)),
            pl.BlockSpec((None, ts, RET_DV), lambda b, h, i: (b, i, h)),
            pl.BlockSpec((None, ts, RET_DV), lambda b, h, i: (b, i, h)),
            pl.BlockSpec((None, None, n_c, RET_DK, RET_DV), lambda b, h, i: (b, h, i, 0, 0)),
            pl.BlockSpec((1, RET_DV), lambda b, h, i: (0, h)),
        ],
        out_specs=pl.BlockSpec((None, ts, RET_DV), lambda b, h, i: (b, i, h)),
        out_shape=jax.ShapeDtypeStruct((B, S, RET_V_WIDTH), BF16),
        scratch_shapes=[pltpu.VMEM((RET_DK, RET_DV), F32)],
        compiler_params=_params("parallel", "parallel", "arbitrary"),
        name="l1_retention",
    )(lgf, lgb, q, kT, v, sg, sb, gn.reshape(1, RET_V_WIDTH))


def _axial_rope_tables(seq_len, head_dim, sin_sign=None):
    rows = seq_len // GRID_W
    axis_dim = head_dim // 2
    inv_freq = ROPE_THETA ** (-jnp.arange(0, axis_dim, 2, dtype=F32) / axis_dim)
    freq = jnp.tile(inv_freq, LANES // inv_freq.shape[0])
    sign = 1.0 if sin_sign is None else sin_sign

    def table(n):
        ang = jnp.arange(n, dtype=F32)[:, None] * freq
        return jnp.stack([jnp.cos(ang), jnp.sin(ang) * sign])

    return table(rows), table(GRID_W)


def _rope_tile(row_tab, col_tab, head_dim):
    axis_dim = head_dim // 2
    lane = lax.broadcasted_iota(jnp.int32, col_tab.shape, 1)
    is_row = (lane % axis_dim) < (axis_dim // 2)
    pieces = [jnp.where(is_row, jnp.broadcast_to(row_tab[g:g + 1, :], col_tab.shape), col_tab)
              for g in range(row_tab.shape[0])]
    return pieces[0] if len(pieces) == 1 else jnp.concatenate(pieces, axis=0)


def kernel(x, l0_w_in, l0_q_norm_g, l0_k_norm_g, l0_dw_w, l0_dw_b, l0_conv_norm_g, l0_conv_norm_b,
           l0_w_out, l0_ln_mix_g, l0_ln_mix_b, l0_ffn_w_gate, l0_ffn_w_up, l0_ffn_w_down,
           l0_ln_ffn_g, l0_ln_ffn_b, l1_w_in, l1_log_decay_fwd, l1_log_decay_bwd, l1_ret_norm_g,
           l1_w_out, l1_ln_mix_g, l1_ln_mix_b, l1_ffn_w_gate, l1_ffn_w_up, l1_ffn_w_down,
           l1_ln_ffn_g, l1_ln_ffn_b):
    B, S, D = x.shape
    M = B * S
    bf = lambda w: w.astype(BF16)

    rotate_sign = jnp.where(jnp.arange(LANES) % HEAD_DIM < HEAD_DIM // 2, -1.0, 1.0).astype(F32)
    rope_a = _axial_rope_tables(S, HEAD_DIM, rotate_sign)
    rope_r = _axial_rope_tables(S, RET_DK)
    head_id = jnp.arange(ATTN_WIDTH) // HEAD_DIM
    grp = (head_id[:, None] == head_id[None, :]).astype(BF16)

    qT, k, vT, z = _l0_in_proj(
        x, bf(l0_w_in),
        jnp.tile(l0_q_norm_g, ATTN_HEADS).reshape(1, ATTN_WIDTH),
        jnp.tile(l0_k_norm_g, ATTN_KV_HEADS).reshape(1, KV_WIDTH),
        *rope_a, grp)
    attn = _attention(qT, k, vT)
    conv = _conformer_conv(z, l0_dw_w, l0_dw_b)
    x2d = x.reshape(M, D)
    x2d = _mix_out_ffn([attn.reshape(M, ATTN_WIDTH), conv.reshape(M, CONV_CH)], x2d, bf(l0_w_out),
                       l0_ln_mix_g, l0_ln_mix_b, bf(l0_ffn_w_gate), bf(l0_ffn_w_up), bf(l0_ffn_w_down),
                       l0_ln_ffn_g, l0_ln_ffn_b, last_part_norm=(l0_conv_norm_g, l0_conv_norm_b))

    q, kT, v, sg, sb = _l1_in_proj(x2d.reshape(B, S, D), bf(l1_w_in), *rope_r, l1_log_decay_bwd)
    y = _retention(q, kT, v, sg, sb, l1_log_decay_fwd, l1_log_decay_bwd, l1_ret_norm_g)
    x2d = _mix_out_ffn([y.reshape(M, RET_V_WIDTH)], x2d, bf(l1_w_out), l1_ln_mix_g, l1_ln_mix_b,
                       bf(l1_ffn_w_gate), bf(l1_ffn_w_up), bf(l1_ffn_w_down), l1_ln_ffn_g, l1_ln_ffn_b)
    return x2d.reshape(B, S, D)
```
